```python
import math
import jax, jax.numpy as jnp
from jax import lax
import numpy as np

D_MODEL = 1024
BATCH = 4
SEQ = 4096
DEPTH = 1

D_MIX = D_MODEL
ATTN_WIDTH = D_MIX // 2
CONV_WIDTH = D_MIX - ATTN_WIDTH
N_HEADS = 8
HEAD_DIM = ATTN_WIDTH // N_HEADS
CONV_KERNEL = 31
MOBA_BLOCK = 256
MOBA_TOPK = 3
Q_CHUNK = 64
ROPE_THETA = 10000.0
D_FF = -(-8 * D_MODEL // (3 * 256)) * 256
EPS = 1e-6
D_IN = 3 * ATTN_WIDTH + 2 * CONV_WIDTH

kernel_name = "hymba_conformer_moba_swiglu"


def rms_norm(x, g):
    xf = x.astype(jnp.float32)
    y = xf * lax.rsqrt(jnp.mean(xf * xf, axis=-1, keepdims=True) + EPS)
    return (y * g.astype(jnp.float32)).astype(x.dtype)


def layer_norm(x, g, b):
    xf = x.astype(jnp.float32)
    mu = jnp.mean(xf, axis=-1, keepdims=True)
    var = jnp.mean(jnp.square(xf - mu), axis=-1, keepdims=True)
    y = (xf - mu) * lax.rsqrt(var + EPS)
    return (y * g.astype(jnp.float32) + b.astype(jnp.float32)).astype(x.dtype)


def rope_tables(seq_len):
    pos = jnp.arange(seq_len, dtype=jnp.float32)
    inv_freq = ROPE_THETA ** (-jnp.arange(0, HEAD_DIM, 2, dtype=jnp.float32) / HEAD_DIM)
    ang = pos[:, None] * inv_freq[None, :]
    ang = jnp.concatenate([ang, ang], axis=-1)
    return jnp.cos(ang)[:, None, :], jnp.sin(ang)[:, None, :]


def apply_rope(x, cos, sin):
    xf = x.astype(jnp.float32)
    x1, x2 = jnp.split(xf, 2, axis=-1)
    rot = jnp.concatenate([-x2, x1], axis=-1)
    return (xf * cos + rot * sin).astype(x.dtype)


def moba_attention(q, k, v):
    B, H, S, Dh = q.shape
    n_blocks = -(-S // MOBA_BLOCK)
    pad = n_blocks * MOBA_BLOCK - S
    kp = jnp.pad(k, ((0, 0), (0, 0), (0, pad), (0, 0)))
    vp = jnp.pad(v, ((0, 0), (0, 0), (0, pad), (0, 0)))
    kb = kp.reshape(B, H, n_blocks, MOBA_BLOCK, Dh)
    vb = vp.reshape(B, H, n_blocks, MOBA_BLOCK, Dh)
    k_mean = jnp.mean(kb.astype(jnp.float32), axis=3)
    topk = min(MOBA_TOPK, n_blocks)
    scale = HEAD_DIM ** -0.5
    b_idx = jnp.arange(B)[:, None, None, None]
    h_idx = jnp.arange(H)[None, :, None, None]
    key_off = jnp.arange(MOBA_BLOCK)
    block_ids = jnp.arange(n_blocks)
    n_chunks = S // Q_CHUNK

    def chunk_fn(c):
        start = c * Q_CHUNK
        qc = lax.dynamic_slice_in_dim(q, start, Q_CHUNK, axis=2)
        q_pos = start + jnp.arange(Q_CHUNK)
        own = start // MOBA_BLOCK
        gate = jnp.einsum('bhqd,bhnd->bhqn', qc.astype(jnp.float32), k_mean)
        gate = jnp.where((block_ids < own)[None, None, None, :], gate, -jnp.inf)
        _, top_idx = lax.top_k(gate, topk)
        sel_valid = top_idx < own
        k_sel = kb[b_idx, h_idx, top_idx]
        v_sel = vb[b_idx, h_idx, top_idx]
        s_sel = jnp.einsum('bhqd,bhqnkd->bhqnk', qc, k_sel).astype(jnp.float32) * scale
        s_sel = jnp.where(sel_valid[..., None], s_sel, -jnp.inf)
        s_sel = s_sel.reshape(B, H, Q_CHUNK, topk * MOBA_BLOCK)
        k_own = lax.dynamic_slice_in_dim(kp, own * MOBA_BLOCK, MOBA_BLOCK, axis=2)
        v_own = lax.dynamic_slice_in_dim(vp, own * MOBA_BLOCK, MOBA_BLOCK, axis=2)
        s_own = jnp.einsum('bhqd,bhkd->bhqk', qc, k_own).astype(jnp.float32) * scale
        own_pos = own * MOBA_BLOCK + key_off
        s_own = jnp.where((own_pos[None, :] <= q_pos[:, None])[None, None], s_own, -jnp.inf)
        p = jax.nn.softmax(jnp.concatenate([s_own, s_sel], axis=-1), axis=-1)
        p_own = p[..., :MOBA_BLOCK].astype(v.dtype)
        p_sel = p[..., MOBA_BLOCK:].reshape(B, H, Q_CHUNK, topk, MOBA_BLOCK).astype(v.dtype)
        out = (jnp.einsum('bhqk,bhkd->bhqd', p_own, v_own)
               + jnp.einsum('bhqnk,bhqnkd->bhqd', p_sel, v_sel))
        return out

    outs = lax.map(chunk_fn, jnp.arange(n_chunks))
    return outs.transpose(1, 2, 0, 3, 4).reshape(B, H, S, Dh)


def conformer_conv(u, glu_b, dw_w, dw_b, ln_g, ln_b):
    u = u + glu_b.astype(u.dtype)
    a, g = jnp.split(u, 2, axis=-1)
    h = a * jax.nn.sigmoid(g)
    h = jnp.pad(h, ((0, 0), (CONV_KERNEL - 1, 0), (0, 0)))
    h = lax.conv_general_dilated(
        h, dw_w[:, None, :].astype(h.dtype), window_strides=(1,), padding='VALID',
        dimension_numbers=('NWC', 'WIO', 'NWC'), feature_group_count=CONV_WIDTH)
    h = h + dw_b.astype(h.dtype)
    h = layer_norm(h, ln_g, ln_b)
    return jax.nn.silu(h)


def setup_inputs(seed: int = 0) -> dict:
    key = jax.random.key(seed)
    ks = jax.random.split(key, 16)
    f32 = jnp.float32
    L = DEPTH

    def nrm(k, shape, scale):
        return jax.random.normal(k, shape, f32) * scale

    return {
        "x": jax.random.normal(ks[0], (BATCH, SEQ, D_MODEL), f32),
        "norm1_g": 1.0 + nrm(ks[1], (L, D_MODEL), 0.02),
        "w_in": nrm(ks[2], (L, D_MODEL, D_IN), D_MODEL ** -0.5),
        "glu_b": nrm(ks[3], (L, 2 * CONV_WIDTH), 0.02),
        "q_norm_g": 1.0 + nrm(ks[4], (L, HEAD_DIM), 0.02),
        "k_norm_g": 1.0 + nrm(ks[5], (L, HEAD_DIM), 0.02),
        "dw_w": nrm(ks[6], (L, CONV_KERNEL, CONV_WIDTH), CONV_KERNEL ** -0.5),
        "dw_b": nrm(ks[7], (L, CONV_WIDTH), 0.02),
        "conv_ln_g": 1.0 + nrm(ks[8], (L, CONV_WIDTH), 0.02),
        "conv_ln_b": nrm(ks[9], (L, CONV_WIDTH), 0.02),
        "w_out": nrm(ks[10], (L, D_MIX, D_MODEL), D_MIX ** -0.5),
        "norm2_g": 1.0 + nrm(ks[11], (L, D_MODEL), 0.02),
        "w_gate": nrm(ks[12], (L, D_MODEL, D_FF), D_MODEL ** -0.5),
        "w_up": nrm(ks[13], (L, D_MODEL, D_FF), D_MODEL ** -0.5),
        "w_down": nrm(ks[14], (L, D_FF, D_MODEL), D_FF ** -0.5),
    }


def reference(x, norm1_g, w_in, glu_b, q_norm_g, k_norm_g, dw_w, dw_b, conv_ln_g,
              conv_ln_b, w_out, norm2_g, w_gate, w_up, w_down):
    B, S, _ = x.shape
    cos, sin = rope_tables(S)
    for l in range(DEPTH):
        h = rms_norm(x, norm1_g[l])
        proj = h @ w_in[l]
        q, k, v, u = jnp.split(proj, [ATTN_WIDTH, 2 * ATTN_WIDTH, 3 * ATTN_WIDTH], axis=-1)
        q = q.reshape(B, S, N_HEADS, HEAD_DIM)
        k = k.reshape(B, S, N_HEADS, HEAD_DIM)
        v = v.reshape(B, S, N_HEADS, HEAD_DIM)
        q = apply_rope(rms_norm(q, q_norm_g[l]), cos, sin)
        k = apply_rope(rms_norm(k, k_norm_g[l]), cos, sin)
        attn = moba_attention(q.transpose(0, 2, 1, 3), k.transpose(0, 2, 1, 3),
                              v.transpose(0, 2, 1, 3))
        attn = attn.transpose(0, 2, 1, 3).reshape(B, S, ATTN_WIDTH)
        conv = conformer_conv(u, glu_b[l], dw_w[l], dw_b[l], conv_ln_g[l], conv_ln_b[l])
        x = x + jnp.concatenate([attn, conv], axis=-1) @ w_out[l]
        h = rms_norm(x, norm2_g[l])
        x = x + (jax.nn.silu(h @ w_gate[l]) * (h @ w_up[l])) @ w_down[l]
    return x
```

```python
import functools

import jax
import jax.numpy as jnp
from jax import lax
from jax.experimental import pallas as pl
from jax.experimental.pallas import tpu as pltpu

D_MODEL = 1024
ATTN_WIDTH = 512
CONV_WIDTH = 512
N_HEADS = 8
HEAD_DIM = 64
CONV_KERNEL = 31
MOBA_BLOCK = 256
MOBA_TOPK = 3
ROPE_THETA = 10000.0
D_FF = 2816
EPS = 1e-6
D_IN = 3 * ATTN_WIDTH + 2 * CONV_WIDTH

LANES = 128
HEADS_PER_VREG = LANES // HEAD_DIM
V_ROWS = 80
CONV_HALO = 32
NEG_BIG = -1e30
POS_BIG = 1e30

TM_PROJ = 512
TM_OUT = 512
FF_CHUNK = 1408
VMEM_LIMIT = 56 * 1024 * 1024

_BF16 = jnp.bfloat16
_F32 = jnp.float32


def _dot(a, b):
    return jnp.dot(a, b, preferred_element_type=_F32)


def _proj_kernel(x_ref, g1_ref, win_ref, glub_ref, gq_ref, gk_ref, gmat_ref, cos_ref, sin_ref,
                 qT_ref, k_ref, vT_ref, kmean_ref, hg_ref):
    tm = x_ref.shape[1]
    nib = tm // MOBA_BLOCK
    x = x_ref[0]
    ms = jnp.mean(x * x, axis=-1, keepdims=True)
    h = (x * lax.rsqrt(ms + EPS) * g1_ref[...]).astype(_BF16)

    cos = cos_ref[...]
    sin = sin_ref[...]
    lane = lax.broadcasted_iota(jnp.int32, (tm, LANES), 1)
    first_half = (lane & (HEAD_DIM // 2)) == 0

    def head_norm_rope(p, g_ref):
        msq = _dot((p * p).astype(_BF16), gmat_ref[...])
        pn = p * lax.rsqrt(msq + EPS) * g_ref[...]
        outs = []
        for c in range(ATTN_WIDTH // LANES):
            xc = pn[:, c * LANES:(c + 1) * LANES]
            partner = jnp.where(first_half,
                                pltpu.roll(xc, LANES - HEAD_DIM // 2, 1),
                                pltpu.roll(xc, HEAD_DIM // 2, 1))
            outs.append(xc * cos + partner * sin)
        return jnp.concatenate(outs, axis=1)

    aw = ATTN_WIDTH
    q = head_norm_rope(_dot(h, win_ref[:, 0:aw]), gq_ref) * (HEAD_DIM ** -0.5)
    k = head_norm_rope(_dot(h, win_ref[:, aw:2 * aw]), gk_ref)
    v = _dot(h, win_ref[:, 2 * aw:3 * aw])

    k_ref[0] = k.astype(_BF16)
    for ib in range(nib):
        kmean_ref[0, 0, ib:ib + 1, :] = jnp.mean(
            k[ib * MOBA_BLOCK:(ib + 1) * MOBA_BLOCK], axis=0, keepdims=True)

    qT = q.T.astype(_BF16)
    vT = v.T.astype(_BF16)
    zeros = jnp.zeros((HEAD_DIM, MOBA_BLOCK), _BF16)
    pad_rows = lax.broadcasted_iota(jnp.int32, (V_ROWS - HEAD_DIM, MOBA_BLOCK), 0)
    ones_row = jnp.where(pad_rows == 0, 1.0, 0.0).astype(_BF16)
    for hd in range(N_HEADS):
        lo = (hd % HEADS_PER_VREG) * HEAD_DIM
        for ib in range(nib):
            cols = slice(ib * MOBA_BLOCK, (ib + 1) * MOBA_BLOCK)
            rows = slice(hd * HEAD_DIM, (hd + 1) * HEAD_DIM)
            qT_ref[0, hd, ib, lo:lo + HEAD_DIM, :] = qT[rows, cols]
            qT_ref[0, hd, ib, HEAD_DIM - lo:2 * HEAD_DIM - lo, :] = zeros
            vT_ref[0, hd, ib, 0:HEAD_DIM, :] = vT[rows, cols]
            vT_ref[0, hd, ib, HEAD_DIM:V_ROWS, :] = ones_row

    cw = CONV_WIDTH
    a = _dot(h, win_ref[:, 3 * aw:3 * aw + cw]) + glub_ref[:, 0:cw]
    g = _dot(h, win_ref[:, 3 * aw + cw:3 * aw + 2 * cw]) + glub_ref[:, cw:2 * cw]
    hg_ref[0] = a * jax.nn.sigmoid(g)


def _proj_call(x, g1, win, glub, gq, gk, gmat, cos2, sin2):
    B, S, D = x.shape
    tm = TM_PROJ
    nib = tm // MOBA_BLOCK
    nb = S // MOBA_BLOCK
    const = lambda b, t: (0, 0)
    return pl.pallas_call(
        _proj_kernel,
        grid=(B, S // tm),
        in_specs=[
            pl.BlockSpec((1, tm, D), lambda b, t: (b, t, 0)),
            pl.BlockSpec((1, D), const),
            pl.BlockSpec((D, D_IN), const),
            pl.BlockSpec((1, 2 * CONV_WIDTH), const),
            pl.BlockSpec((1, ATTN_WIDTH), const),
            pl.BlockSpec((1, ATTN_WIDTH), const),
            pl.BlockSpec((ATTN_WIDTH, ATTN_WIDTH), const),
            pl.BlockSpec((tm, LANES), lambda b, t: (t, 0)),
            pl.BlockSpec((tm, LANES), lambda b, t: (t, 0)),
        ],
        out_specs=[
            pl.BlockSpec((1, N_HEADS, nib, LANES, MOBA_BLOCK), lambda b, t: (b, 0, t, 0, 0)),
            pl.BlockSpec((1, tm, ATTN_WIDTH), lambda b, t: (b, t, 0)),
            pl.BlockSpec((1, N_HEADS, nib, V_ROWS, MOBA_BLOCK), lambda b, t: (b, 0, t, 0, 0)),
            pl.BlockSpec((1, 1, nib, ATTN_WIDTH), lambda b, t: (b, t, 0, 0)),
            pl.BlockSpec((1, tm, CONV_WIDTH), lambda b, t: (b, t, 0)),
        ],
        out_shape=[
            jax.ShapeDtypeStruct((B, N_HEADS, nb, LANES, MOBA_BLOCK), _BF16),
            jax.ShapeDtypeStruct((B, S, ATTN_WIDTH), _BF16),
            jax.ShapeDtypeStruct((B, N_HEADS, nb, V_ROWS, MOBA_BLOCK), _BF16),
            jax.ShapeDtypeStruct((B, S // tm, nib, ATTN_WIDTH), _F32),
            jax.ShapeDtypeStruct((B, S, CONV_WIDTH), _F32),
        ],
        compiler_params=pltpu.CompilerParams(
            dimension_semantics=("arbitrary", "arbitrary"), vmem_limit_bytes=VMEM_LIMIT),
        name="moba_proj",
    )(x, g1, win, glub, gq, gk, gmat, cos2, sin2)


def _attn_kernel(qT_ref, k_ref, vT_ref, kmean_ref, o_ref, acc_ref, m_ref, sel_ref):
    i = pl.program_id(1)
    nb = kmean_ref.shape[1]
    blk = MOBA_BLOCK
    row = lax.broadcasted_iota(jnp.int32, (blk, blk), 0)
    col = lax.broadcasted_iota(jnp.int32, (blk, blk), 1)
    causal = row <= col
    blk_id = lax.broadcasted_iota(jnp.int32, (nb, blk), 0)

    def k_block(j, pair):
        start = pl.multiple_of(j * blk, blk)
        return k_ref[0, pl.ds(start, blk), pair * LANES:(pair + 1) * LANES]

    for hd in range(N_HEADS):
        pair = hd // HEADS_PER_VREG
        qh = qT_ref[0, hd, 0]
        km = kmean_ref[0, :, pair * LANES:(pair + 1) * LANES].astype(_BF16)
        gate = _dot(km, qh)
        cnt = jnp.zeros((nb, blk), _F32)
        for m in range(nb):
            gm = gate[m:m + 1, :]
            beats = (gm > gate) | ((gm == gate) & (blk_id > m))
            cnt = cnt + jnp.where(beats, jnp.where(i > m, 1.0, 0.0), 0.0)
        sel = jnp.where((blk_id < i) & (cnt < MOBA_TOPK), 1.0, 0.0)
        for n in range(nb):
            sel_ref[hd, n] = sel[n:n + 1, :]
        s = jnp.where(causal, _dot(k_block(i, pair), qh), NEG_BIG)
        m0 = jnp.max(s, axis=0, keepdims=True)
        p = jnp.exp(s - m0)
        acc_ref[hd] = _dot(vT_ref[0, hd, i], p.astype(_BF16))
        m_ref[hd] = m0

    def past_block(j, carry):
        for hd in range(N_HEADS):
            pair = hd // HEADS_PER_VREG
            s = _dot(k_block(j, pair), qT_ref[0, hd, 0])
            on = sel_ref[hd, j] > 0.5
            m_old = m_ref[hd]
            m_new = jnp.maximum(m_old, jnp.where(on, jnp.max(s, axis=0, keepdims=True), NEG_BIG))
            alpha = jnp.exp(m_old - m_new)
            p = jnp.exp(s - jnp.where(on, m_new, POS_BIG))
            acc_ref[hd] = alpha * acc_ref[hd] + _dot(vT_ref[0, hd, j], p.astype(_BF16))
            m_ref[hd] = m_new
        return carry

    lax.fori_loop(0, i, past_block, 0)

    for hd in range(N_HEADS):
        acc = acc_ref[hd]
        denom = acc[HEAD_DIM:HEAD_DIM + 1, :]
        o_ref[0, hd, 0] = (acc[0:HEAD_DIM, :] * (1.0 / denom)).astype(_BF16)


def _attn_call(qT, k, vT, kmean):
    B, H, nb, _, blk = qT.shape
    S = k.shape[1]
    return pl.pallas_call(
        _attn_kernel,
        grid=(B, nb),
        in_specs=[
            pl.BlockSpec((1, H, 1, LANES, blk), lambda b, i: (b, 0, i, 0, 0)),
            pl.BlockSpec((1, S, ATTN_WIDTH), lambda b, i: (b, 0, 0)),
            pl.BlockSpec((1, H, nb, V_ROWS, blk), lambda b, i: (b, 0, 0, 0, 0)),
            pl.BlockSpec((1, nb, ATTN_WIDTH), lambda b, i: (b, 0, 0)),
        ],
        out_specs=pl.BlockSpec((1, H, 1, HEAD_DIM, blk), lambda b, i: (b, 0, i, 0, 0)),
        out_shape=jax.ShapeDtypeStruct((B, H, nb, HEAD_DIM, blk), _BF16),
        scratch_shapes=[
            pltpu.VMEM((H, V_ROWS, blk), _F32),
            pltpu.VMEM((H, 1, blk), _F32),
            pltpu.VMEM((H, nb, 1, blk), _F32),
        ],
        compiler_params=pltpu.CompilerParams(
            dimension_semantics=("arbitrary", "arbitrary"), vmem_limit_bytes=VMEM_LIMIT),
        name="moba_attn",
    )(qT, k, vT, kmean)


def _out_kernel(x_ref, oT_ref, hg_ref, halo_ref, dww_ref, dwb_ref, lng_ref, lnb_ref, wout_ref,
                g2_ref, wg_ref, wu_ref, wd_ref, y_ref, hbuf_ref):
    tm = x_ref.shape[1]
    nib = tm // MOBA_BLOCK
    t = pl.program_id(1)

    hbuf_ref[0:CONV_HALO, :] = halo_ref[0] * jnp.where(t > 0, 1.0, 0.0)
    hbuf_ref[CONV_HALO:CONV_HALO + tm, :] = hg_ref[0]
    first = CONV_HALO - (CONV_KERNEL - 1)
    conv = dww_ref[0:1, :] * hbuf_ref[first:first + tm, :]
    for kk in range(1, CONV_KERNEL):
        conv = conv + dww_ref[kk:kk + 1, :] * hbuf_ref[first + kk:first + kk + tm, :]
    conv = conv + dwb_ref[...]
    mu = jnp.mean(conv, axis=-1, keepdims=True)
    cen = conv - mu
    var = jnp.mean(cen * cen, axis=-1, keepdims=True)
    cn = cen * lax.rsqrt(var + EPS) * lng_ref[...] + lnb_ref[...]
    cact = (cn * jax.nn.sigmoid(cn)).astype(_BF16)

    mix = _dot(cact, wout_ref[ATTN_WIDTH:ATTN_WIDTH + CONV_WIDTH, :])
    attn_parts = []
    for ib in range(nib):
        oT = oT_ref[0, :, ib].reshape(ATTN_WIDTH, MOBA_BLOCK)
        attn_parts.append(lax.dot_general(oT, wout_ref[0:ATTN_WIDTH, :], (((0,), (0,)), ((), ())),
                                          preferred_element_type=_F32))
    x1 = x_ref[0] + (mix + jnp.concatenate(attn_parts, axis=0))

    ms = jnp.mean(x1 * x1, axis=-1, keepdims=True)
    h2 = (x1 * lax.rsqrt(ms + EPS) * g2_ref[...]).astype(_BF16)
    y = x1
    for c in range(D_FF // FF_CHUNK):
        cs = slice(c * FF_CHUNK, (c + 1) * FF_CHUNK)
        gt = _dot(h2, wg_ref[:, cs])
        up = _dot(h2, wu_ref[:, cs])
        act = (gt * jax.nn.sigmoid(gt) * up).astype(_BF16)
        y = y + _dot(act, wd_ref[cs, :])
    y_ref[0] = y


def _out_call(x, oT, hg, dww, dwb, lng, lnb, wout, g2, wg, wu, wd):
    B, S, D = x.shape
    tm = TM_OUT
    nib = tm // MOBA_BLOCK
    H = N_HEADS
    const = lambda b, t: (0, 0)
    resident = functools.partial(pl.BlockSpec, pipeline_mode=pl.Buffered(1))
    halo_blocks = tm // CONV_HALO
    return pl.pallas_call(
        _out_kernel,
        grid=(B, S // tm),
        in_specs=[
            pl.BlockSpec((1, tm, D), lambda b, t: (b, t, 0)),
            pl.BlockSpec((1, H, nib, HEAD_DIM, MOBA_BLOCK), lambda b, t: (b, 0, t, 0, 0)),
            pl.BlockSpec((1, tm, CONV_WIDTH), lambda b, t: (b, t, 0)),
            pl.BlockSpec((1, CONV_HALO, CONV_WIDTH),
                         lambda b, t: (b, jnp.maximum(t * halo_blocks - 1, 0), 0)),
            pl.BlockSpec((CONV_KERNEL, CONV_WIDTH), const),
            pl.BlockSpec((1, CONV_WIDTH), const),
            pl.BlockSpec((1, CONV_WIDTH), const),
            pl.BlockSpec((1, CONV_WIDTH), const),
            resident((D, D), const),
            pl.BlockSpec((1, D), const),
            resident((D, D_FF), const),
            resident((D, D_FF), const),
            resident((D_FF, D), const),
        ],
        out_specs=pl.BlockSpec((1, tm, D), lambda b, t: (b, t, 0)),
        out_shape=jax.ShapeDtypeStruct((B, S, D), _F32),
        scratch_shapes=[pltpu.VMEM((CONV_HALO + tm, CONV_WIDTH), _F32)],
        compiler_params=pltpu.CompilerParams(
            dimension_semantics=("arbitrary", "arbitrary"), vmem_limit_bytes=VMEM_LIMIT),
        name="moba_out_ffn",
    )(x, oT, hg, hg, dww, dwb, lng, lnb, wout, g2, wg, wu, wd)


def _rope_tables(seq_len):
    pos = jnp.arange(seq_len, dtype=_F32)
    inv_freq = ROPE_THETA ** (-jnp.arange(0, HEAD_DIM, 2, dtype=_F32) / HEAD_DIM)
    ang = pos[:, None] * inv_freq[None, :]
    ang = jnp.concatenate([ang, ang], axis=-1)
    sign = jnp.where(jnp.arange(HEAD_DIM) < HEAD_DIM // 2, -1.0, 1.0).astype(_F32)
    cos2 = jnp.tile(jnp.cos(ang), (1, HEADS_PER_VREG))
    sin2 = jnp.tile(jnp.sin(ang) * sign[None, :], (1, HEADS_PER_VREG))
    return cos2, sin2


def _layer(x, norm1_g, w_in, glu_b, q_norm_g, k_norm_g, dw_w, dw_b, conv_ln_g, conv_ln_b,
           w_out, norm2_g, w_gate, w_up, w_down, cos2, sin2, gmat):
    B, S, _ = x.shape
    row = lambda a: a.reshape(1, -1)
    qT, k, vT, kmean, hg = _proj_call(
        x, row(norm1_g), w_in.astype(_BF16), row(glu_b),
        row(jnp.tile(q_norm_g, N_HEADS)), row(jnp.tile(k_norm_g, N_HEADS)), gmat, cos2, sin2)
    kmean = kmean.reshape(B, S // MOBA_BLOCK, ATTN_WIDTH)
    oT = _attn_call(qT, k, vT, kmean)
    return _out_call(x, oT, hg, dw_w, row(dw_b), row(conv_ln_g), row(conv_ln_b),
                     w_out.astype(_BF16), row(norm2_g), w_gate.astype(_BF16),
                     w_up.astype(_BF16), w_down.astype(_BF16))


def kernel(x, norm1_g, w_in, glu_b, q_norm_g, k_norm_g, dw_w, dw_b, conv_ln_g, conv_ln_b, w_out,
           norm2_g, w_gate, w_up, w_down):
    S = x.shape[1]
    cos2, sin2 = _rope_tables(S)
    head_of = jnp.arange(ATTN_WIDTH) // HEAD_DIM
    gmat = jnp.where(head_of[:, None] == head_of[None, :], 1.0 / HEAD_DIM, 0.0).astype(_BF16)
    for l in range(norm1_g.shape[0]):
        x = _layer(x, norm1_g[l], w_in[l], glu_b[l], q_norm_g[l], k_norm_g[l], dw_w[l], dw_b[l],
                   conv_ln_g[l], conv_ln_b[l], w_out[l], norm2_g[l], w_gate[l], w_up[l], w_down[l],
                   cos2, sin2, gmat)
    return x
```

```python
import functools

import jax
import jax.numpy as jnp
from jax import lax
from jax.experimental import pallas as pl
from jax.experimental.pallas import tpu as pltpu

D_MODEL = 1024
ATTN_WIDTH = 512
CONV_WIDTH = 512
N_HEADS = 8
HEAD_DIM = 64
CONV_KERNEL = 31
MOBA_BLOCK = 256
MOBA_TOPK = 3
ROPE_THETA = 10000.0
D_FF = 2816
EPS = 1e-6
D_IN = 3 * ATTN_WIDTH + 2 * CONV_WIDTH

LANES = 128
HEADS_PER_VREG = LANES // HEAD_DIM
V_ROWS = 80
EXP_ROWS = 16
CONV_HALO = 32
NEG_BIG = -1e30
POS_BIG = 1e30

TM_PROJ = 512
TM_OUT = 512
FF_CHUNK = 1024
SUBLANES = 8
CONV_ROWS = 16
VMEM_LIMIT = 56 * 1024 * 1024

_BF16 = jnp.bfloat16
_F32 = jnp.float32


def _dot(a, b):
    return jnp.dot(a, b, preferred_element_type=_F32)


def _proj_kernel(x_ref, g1_ref, win_ref, glub_ref, gq_ref, gk_ref, gmat_ref, cos_ref, sin_ref,
                 qT_ref, k_ref, vT_ref, kmean_ref, hg_ref):
    tm = x_ref.shape[1]
    nib = tm // MOBA_BLOCK
    x = x_ref[0]
    ms = jnp.mean(x * x, axis=-1, keepdims=True)
    h = (x * lax.rsqrt(ms + EPS) * g1_ref[...]).astype(_BF16)

    cos = cos_ref[...]
    sin = sin_ref[...]
    lane = lax.broadcasted_iota(jnp.int32, (tm, LANES), 1)
    first_half = (lane & (HEAD_DIM // 2)) == 0

    def head_norm_rope(p, g_ref):
        msq = _dot((p * p).astype(_BF16), gmat_ref[...])
        pn = p * lax.rsqrt(msq + EPS) * g_ref[...]
        outs = []
        for c in range(ATTN_WIDTH // LANES):
            xc = pn[:, c * LANES:(c + 1) * LANES]
            partner = jnp.where(first_half,
                                pltpu.roll(xc, LANES - HEAD_DIM // 2, 1),
                                pltpu.roll(xc, HEAD_DIM // 2, 1))
            outs.append(xc * cos + partner * sin)
        return jnp.concatenate(outs, axis=1)

    aw = ATTN_WIDTH
    q = head_norm_rope(_dot(h, win_ref[:, 0:aw]), gq_ref) * (HEAD_DIM ** -0.5)
    k = head_norm_rope(_dot(h, win_ref[:, aw:2 * aw]), gk_ref)
    v = _dot(h, win_ref[:, 2 * aw:3 * aw])

    k_ref[0] = k.astype(_BF16)
    for ib in range(nib):
        kmean_ref[0, 0, ib:ib + 1, :] = jnp.mean(
            k[ib * MOBA_BLOCK:(ib + 1) * MOBA_BLOCK], axis=0, keepdims=True)

    qT = q.T.astype(_BF16)
    vT = v.T.astype(_BF16)
    zeros = jnp.zeros((HEAD_DIM, MOBA_BLOCK), _BF16)
    pad_rows = lax.broadcasted_iota(jnp.int32, (V_ROWS - HEAD_DIM, MOBA_BLOCK), 0)
    ones_row = jnp.where(pad_rows == 0, 1.0, 0.0).astype(_BF16)
    for hd in range(N_HEADS):
        lo = (hd % HEADS_PER_VREG) * HEAD_DIM
        for ib in range(nib):
            cols = slice(ib * MOBA_BLOCK, (ib + 1) * MOBA_BLOCK)
            rows = slice(hd * HEAD_DIM, (hd + 1) * HEAD_DIM)
            qT_ref[0, hd, ib, lo:lo + HEAD_DIM, :] = qT[rows, cols]
            qT_ref[0, hd, ib, HEAD_DIM - lo:2 * HEAD_DIM - lo, :] = zeros
            vT_ref[0, hd, ib, 0:HEAD_DIM, :] = vT[rows, cols]
            vT_ref[0, hd, ib, HEAD_DIM:V_ROWS, :] = ones_row

    cw = CONV_WIDTH
    a = _dot(h, win_ref[:, 3 * aw:3 * aw + cw]) + glub_ref[:, 0:cw]
    g = _dot(h, win_ref[:, 3 * aw + cw:3 * aw + 2 * cw]) + glub_ref[:, cw:2 * cw]
    hg_ref[0] = a * jax.nn.sigmoid(g)


def _proj_call(x, g1, win, glub, gq, gk, gmat, cos2, sin2):
    B, S, D = x.shape
    tm = TM_PROJ
    nib = tm // MOBA_BLOCK
    nb = S // MOBA_BLOCK
    const = lambda b, t: (0, 0)
    return pl.pallas_call(
        _proj_kernel,
        grid=(B, S // tm),
        in_specs=[
            pl.BlockSpec((1, tm, D), lambda b, t: (b, t, 0)),
            pl.BlockSpec((1, D), const),
            pl.BlockSpec((D, D_IN), const, pipeline_mode=pl.Buffered(1)),
            pl.BlockSpec((1, 2 * CONV_WIDTH), const),
            pl.BlockSpec((1, ATTN_WIDTH), const),
            pl.BlockSpec((1, ATTN_WIDTH), const),
            pl.BlockSpec((ATTN_WIDTH, ATTN_WIDTH), const),
            pl.BlockSpec((tm, LANES), lambda b, t: (t, 0)),
            pl.BlockSpec((tm, LANES), lambda b, t: (t, 0)),
        ],
        out_specs=[
            pl.BlockSpec((1, N_HEADS, nib, LANES, MOBA_BLOCK), lambda b, t: (b, 0, t, 0, 0)),
            pl.BlockSpec((1, tm, ATTN_WIDTH), lambda b, t: (b, t, 0)),
            pl.BlockSpec((1, N_HEADS, nib, V_ROWS, MOBA_BLOCK), lambda b, t: (b, 0, t, 0, 0)),
            pl.BlockSpec((1, 1, nib, ATTN_WIDTH), lambda b, t: (b, t, 0, 0)),
            pl.BlockSpec((1, tm, CONV_WIDTH), lambda b, t: (b, t, 0)),
        ],
        out_shape=[
            jax.ShapeDtypeStruct((B, N_HEADS, nb, LANES, MOBA_BLOCK), _BF16),
            jax.ShapeDtypeStruct((B, S, ATTN_WIDTH), _BF16),
            jax.ShapeDtypeStruct((B, N_HEADS, nb, V_ROWS, MOBA_BLOCK), _BF16),
            jax.ShapeDtypeStruct((B, S // tm, nib, ATTN_WIDTH), _F32),
            jax.ShapeDtypeStruct((B, S, CONV_WIDTH), _F32),
        ],
        compiler_params=pltpu.CompilerParams(
            dimension_semantics=("arbitrary", "arbitrary"), vmem_limit_bytes=VMEM_LIMIT),
        name="moba_proj",
    )(x, g1, win, glub, gq, gk, gmat, cos2, sin2)


def _attn_kernel(qT_ref, k_ref, vT_ref, kmean_ref, o_ref,
                 acc_ref, m_ref, sel_ref, s_ref, smax_ref, p_ref):
    i = pl.program_id(1)
    nb = kmean_ref.shape[1]
    blk = MOBA_BLOCK
    row = lax.broadcasted_iota(jnp.int32, (blk, blk), 0)
    col = lax.broadcasted_iota(jnp.int32, (blk, blk), 1)
    causal = row <= col
    blk_id = lax.broadcasted_iota(jnp.int32, (nb, blk), 0)

    def k_block(j, pair):
        start = pl.multiple_of(j * blk, blk)
        return k_ref[0, pl.ds(start, blk), pair * LANES:(pair + 1) * LANES]

    def stage(hd, j, masked):
        pair = hd // HEADS_PER_VREG
        s = _dot(k_block(j, pair), qT_ref[0, hd, 0])
        if masked:
            s = jnp.where(causal, s, NEG_BIG)
        s_ref[hd] = s
        smax_ref[hd] = jnp.max(s, axis=0, keepdims=True)

    def process(hd, t, j):
        on = sel_ref[hd, t] > 0.5
        m_old = m_ref[hd]
        m_new = jnp.maximum(m_old, jnp.where(on, smax_ref[hd], NEG_BIG))
        alpha = jnp.exp(m_old - m_new)
        shift = jnp.broadcast_to(jnp.where(on, m_new, POS_BIG), (EXP_ROWS, blk))
        for r in range(blk // EXP_ROWS):
            rows = slice(r * EXP_ROWS, (r + 1) * EXP_ROWS)
            p_ref[hd, rows, :] = jnp.exp(s_ref[hd, rows, :] - shift).astype(_BF16)
        acc_ref[hd] = alpha * acc_ref[hd] + _dot(vT_ref[0, hd, j], p_ref[hd])
        m_ref[hd] = m_new

    for hd in range(N_HEADS):
        pair = hd // HEADS_PER_VREG
        qh = qT_ref[0, hd, 0]
        km = kmean_ref[0, :, pair * LANES:(pair + 1) * LANES].astype(_BF16)
        gate = _dot(km, qh)
        cnt = jnp.zeros((nb, blk), _F32)
        for m in range(nb):
            gm = gate[m:m + 1, :]
            beats = (gm > gate) | ((gm == gate) & (blk_id > m))
            cnt = cnt + jnp.where(beats, jnp.where(i > m, 1.0, 0.0), 0.0)
        sel = jnp.where((blk_id < i) & (cnt < MOBA_TOPK), 1.0, 0.0)
        sel_ref[hd, 0] = jnp.ones((1, blk), _F32)
        for n in range(nb - 1):
            sel_ref[hd, n + 1] = sel[n:n + 1, :]
        m_ref[hd] = jnp.full((1, blk), NEG_BIG, _F32)
        acc_ref[hd] = jnp.zeros((V_ROWS, blk), _F32)
        stage(hd, i, masked=True)

    def sweep_step(u, carry):
        j_staged = jnp.where(u == 0, i, u - 1)
        for hd in range(N_HEADS):
            process(hd, u, j_staged)
            stage(hd, u, masked=False)
        return carry

    lax.fori_loop(0, i, sweep_step, 0)

    j_last = jnp.maximum(i - 1, 0)
    for hd in range(N_HEADS):
        process(hd, i, j_last)
        acc = acc_ref[hd]
        denom = acc[HEAD_DIM:HEAD_DIM + 1, :]
        o_ref[0, hd, 0] = (acc[0:HEAD_DIM, :] * (1.0 / denom)).astype(_BF16)


def _attn_call(qT, k, vT, kmean):
    B, H, nb, _, blk = qT.shape
    S = k.shape[1]
    return pl.pallas_call(
        _attn_kernel,
        grid=(B, nb),
        in_specs=[
            pl.BlockSpec((1, H, 1, LANES, blk), lambda b, i: (b, 0, i, 0, 0)),
            pl.BlockSpec((1, S, ATTN_WIDTH), lambda b, i: (b, 0, 0)),
            pl.BlockSpec((1, H, nb, V_ROWS, blk), lambda b, i: (b, 0, 0, 0, 0)),
            pl.BlockSpec((1, nb, ATTN_WIDTH), lambda b, i: (b, 0, 0)),
        ],
        out_specs=pl.BlockSpec((1, H, 1, HEAD_DIM, blk), lambda b, i: (b, 0, i, 0, 0)),
        out_shape=jax.ShapeDtypeStruct((B, H, nb, HEAD_DIM, blk), _BF16),
        scratch_shapes=[
            pltpu.VMEM((H, V_ROWS, blk), _F32),
            pltpu.VMEM((H, 1, blk), _F32),
            pltpu.VMEM((H, nb, 1, blk), _F32),
            pltpu.VMEM((H, blk, blk), _F32),
            pltpu.VMEM((H, 1, blk), _F32),
            pltpu.VMEM((H, blk, blk), _BF16),
        ],
        compiler_params=pltpu.CompilerParams(
            dimension_semantics=("arbitrary", "arbitrary"), vmem_limit_bytes=VMEM_LIMIT),
        name="moba_attn",
    )(qT, k, vT, kmean)


def _out_kernel(x_ref, oT_ref, hg_ref, halo_ref, dww_ref, dwb_ref, lng_ref, lnb_ref, wout_ref,
                g2_ref, wg_ref, wu_ref, wd_ref, y_ref, hbuf_ref, hshift_ref, wtap_ref, cact_ref):
    tm = x_ref.shape[1]
    nib = tm // MOBA_BLOCK
    t = pl.program_id(1)

    halo = halo_ref[0]
    hbuf_ref[0:CONV_HALO, :] = jnp.where(t == 0, jnp.zeros_like(halo), halo)
    hbuf_ref[CONV_HALO:CONV_HALO + tm, :] = hg_ref[0]
    span = hshift_ref.shape[1]
    for r in range(1, SUBLANES):
        hshift_ref[r - 1] = hbuf_ref[r:r + span, :]
    for kk in range(CONV_KERNEL):
        wtap_ref[kk] = jnp.broadcast_to(dww_ref[kk:kk + 1, :], (SUBLANES, CONV_WIDTH))
    first = CONV_HALO - (CONV_KERNEL - 1)
    for c0 in range(0, tm, CONV_ROWS):
        accs = [None] * (CONV_ROWS // SUBLANES)
        for kk in range(CONV_KERNEL):
            r = (first + kk) % SUBLANES
            lo = c0 + first + kk - r
            src = hbuf_ref if r == 0 else hshift_ref.at[r - 1]
            wk = wtap_ref[kk]
            for gi in range(len(accs)):
                term = wk * src[lo + gi * SUBLANES:lo + (gi + 1) * SUBLANES, :]
                accs[gi] = term if accs[gi] is None else accs[gi] + term
        conv = jnp.concatenate(accs, axis=0) + dwb_ref[...]
        mu = jnp.mean(conv, axis=-1, keepdims=True)
        cen = conv - mu
        var = jnp.mean(cen * cen, axis=-1, keepdims=True)
        cn = cen * lax.rsqrt(var + EPS) * lng_ref[...] + lnb_ref[...]
        cact_ref[c0:c0 + CONV_ROWS, :] = (cn * jax.nn.sigmoid(cn)).astype(_BF16)

    mix = _dot(cact_ref[...], wout_ref[ATTN_WIDTH:ATTN_WIDTH + CONV_WIDTH, :])
    attn_parts = []
    for ib in range(nib):
        oT = oT_ref[0, :, ib].reshape(ATTN_WIDTH, MOBA_BLOCK)
        attn_parts.append(lax.dot_general(oT, wout_ref[0:ATTN_WIDTH, :], (((0,), (0,)), ((), ())),
                                          preferred_element_type=_F32))
    x1 = x_ref[0] + (mix + jnp.concatenate(attn_parts, axis=0))

    ms = jnp.mean(x1 * x1, axis=-1, keepdims=True)
    h2 = (x1 * lax.rsqrt(ms + EPS) * g2_ref[...]).astype(_BF16)
    y = x1
    for c0 in range(0, D_FF, FF_CHUNK):
        cs = slice(c0, min(c0 + FF_CHUNK, D_FF))
        gt = _dot(h2, wg_ref[:, cs])
        up = _dot(h2, wu_ref[:, cs])
        act = (gt * jax.nn.sigmoid(gt) * up).astype(_BF16)
        y = y + _dot(act, wd_ref[cs, :])
    y_ref[0] = y


def _out_call(x, oT, hg, dww, dwb, lng, lnb, wout, g2, wg, wu, wd):
    B, S, D = x.shape
    tm = TM_OUT
    nib = tm // MOBA_BLOCK
    H = N_HEADS
    const = lambda b, t: (0, 0)
    resident = functools.partial(pl.BlockSpec, pipeline_mode=pl.Buffered(1))
    halo_blocks = tm // CONV_HALO
    return pl.pallas_call(
        _out_kernel,
        grid=(B, S // tm),
        in_specs=[
            pl.BlockSpec((1, tm, D), lambda b, t: (b, t, 0)),
            pl.BlockSpec((1, H, nib, HEAD_DIM, MOBA_BLOCK), lambda b, t: (b, 0, t, 0, 0)),
            pl.BlockSpec((1, tm, CONV_WIDTH), lambda b, t: (b, t, 0)),
            pl.BlockSpec((1, CONV_HALO, CONV_WIDTH),
                         lambda b, t: (b, jnp.maximum(t * halo_blocks - 1, 0), 0)),
            pl.BlockSpec((CONV_KERNEL, CONV_WIDTH), const),
            pl.BlockSpec((1, CONV_WIDTH), const),
            pl.BlockSpec((1, CONV_WIDTH), const),
            pl.BlockSpec((1, CONV_WIDTH), const),
            resident((D, D), const),
            pl.BlockSpec((1, D), const),
            resident((D, D_FF), const),
            resident((D, D_FF), const),
            resident((D_FF, D), const),
        ],
        out_specs=pl.BlockSpec((1, tm, D), lambda b, t: (b, t, 0)),
        out_shape=jax.ShapeDtypeStruct((B, S, D), _F32),
        scratch_shapes=[
            pltpu.VMEM((CONV_HALO + tm, CONV_WIDTH), _F32),
            pltpu.VMEM((SUBLANES - 1, CONV_HALO + tm - SUBLANES, CONV_WIDTH), _F32),
            pltpu.VMEM((CONV_KERNEL, SUBLANES, CONV_WIDTH), _F32),
            pltpu.VMEM((tm, CONV_WIDTH), _BF16),
        ],
        compiler_params=pltpu.CompilerParams(
            dimension_semantics=("arbitrary", "arbitrary"), vmem_limit_bytes=VMEM_LIMIT),
        name="moba_out_ffn",
    )(x, oT, hg, hg, dww, dwb, lng, lnb, wout, g2, wg, wu, wd)


def _rope_tables(seq_len):
    pos = jnp.arange(seq_len, dtype=_F32)
    inv_freq = ROPE_THETA ** (-jnp.arange(0, HEAD_DIM, 2, dtype=_F32) / HEAD_DIM)
    ang = pos[:, None] * inv_freq[None, :]
    ang = jnp.concatenate([ang, ang], axis=-1)
    sign = jnp.where(jnp.arange(HEAD_DIM) < HEAD_DIM // 2, -1.0, 1.0).astype(_F32)
    cos2 = jnp.tile(jnp.cos(ang), (1, HEADS_PER_VREG))
    sin2 = jnp.tile(jnp.sin(ang) * sign[None, :], (1, HEADS_PER_VREG))
    return cos2, sin2


def _layer(x, norm1_g, w_in, glu_b, q_norm_g, k_norm_g, dw_w, dw_b, conv_ln_g, conv_ln_b,
           w_out, norm2_g, w_gate, w_up, w_down, cos2, sin2, gmat):
    B, S, _ = x.shape
    row = lambda a: a.reshape(1, -1)
    qT, k, vT, kmean, hg = _proj_call(
        x, row(norm1_g), w_in.astype(_BF16), row(glu_b),
        row(jnp.tile(q_norm_g, N_HEADS)), row(jnp.tile(k_norm_g, N_HEADS)), gmat, cos2, sin2)
    kmean = kmean.reshape(B, S // MOBA_BLOCK, ATTN_WIDTH)
    oT = _attn_call(qT, k, vT, kmean)
    return _out_call(x, oT, hg, dw_w, row(dw_b), row(conv_ln_g), row(conv_ln_b),
                     w_out.astype(_BF16), row(norm2_g), w_gate.astype(_BF16),
                     w_up.astype(_BF16), w_down.astype(_BF16))


def kernel(x, norm1_g, w_in, glu_b, q_norm_g, k_norm_g, dw_w, dw_b, conv_ln_g, conv_ln_b, w_out,
           norm2_g, w_gate, w_up, w_down):
    S = x.shape[1]
    cos2, sin2 = _rope_tables(S)
    head_of = jnp.arange(ATTN_WIDTH) // HEAD_DIM
    gmat = jnp.where(head_of[:, None] == head_of[None, :], 1.0 / HEAD_DIM, 0.0).astype(_BF16)
    for l in range(norm1_g.shape[0]):
        x = _layer(x, norm1_g[l], w_in[l], glu_b[l], q_norm_g[l], k_norm_g[l], dw_w[l], dw_b[l],
                   conv_ln_g[l], conv_ln_b[l], w_out[l], norm2_g[l], w_gate[l], w_up[l], w_down[l],
                   cos2, sin2, gmat)
    return x
```

```python
import functools

import jax
import jax.numpy as jnp
from jax import lax
from jax.experimental import pallas as pl
from jax.experimental.pallas import tpu as pltpu

D_MODEL = 1024
ATTN_WIDTH = 512
CONV_WIDTH = 512
N_HEADS = 8
HEAD_DIM = 64
CONV_KERNEL = 31
MOBA_BLOCK = 256
MOBA_TOPK = 3
ROPE_THETA = 10000.0
D_FF = 2816
EPS = 1e-6
D_IN = 3 * ATTN_WIDTH + 2 * CONV_WIDTH

LANES = 128
HEADS_PER_VREG = LANES // HEAD_DIM
V_ROWS = 80
EXP_ROWS = 16
CONV_HALO = 32
LOG2_E = 1.4426950408889634
Q_SCALE = HEAD_DIM ** -0.5 * LOG2_E
NEG_BIG = -1e30
POS_BIG = 1e30

TM_PROJ = 512
TM_OUT = 512
FF_CHUNK = 256
SUBLANES = 8
CONV_ROWS = 16
CONV_CHAINS = 4
VMEM_LIMIT = 56 * 1024 * 1024

_BF16 = jnp.bfloat16
_F32 = jnp.float32


def _dot(a, b):
    return jnp.dot(a, b, preferred_element_type=_F32)


def _proj_kernel(x_ref, g1_ref, win_ref, glub_ref, gq_ref, gk_ref, gmat_ref, cos_ref, sin_ref,
                 qT_ref, k_ref, vT_ref, kmean_ref, hg_ref):
    tm = x_ref.shape[1]
    nib = tm // MOBA_BLOCK
    x = x_ref[0]
    ms = jnp.mean(x * x, axis=-1, keepdims=True)
    h = (x * lax.rsqrt(ms + EPS) * g1_ref[...]).astype(_BF16)

    cos = cos_ref[...]
    sin = sin_ref[...]
    lane = lax.broadcasted_iota(jnp.int32, (tm, LANES), 1)
    first_half = (lane & (HEAD_DIM // 2)) == 0

    def head_norm_rope(p, g_ref):
        msq = _dot((p * p).astype(_BF16), gmat_ref[...])
        pn = p * lax.rsqrt(msq + EPS) * g_ref[...]
        outs = []
        for c in range(ATTN_WIDTH // LANES):
            xc = pn[:, c * LANES:(c + 1) * LANES]
            partner = jnp.where(first_half,
                                pltpu.roll(xc, LANES - HEAD_DIM // 2, 1),
                                pltpu.roll(xc, HEAD_DIM // 2, 1))
            outs.append(xc * cos + partner * sin)
        return jnp.concatenate(outs, axis=1)

    aw = ATTN_WIDTH
    q = head_norm_rope(_dot(h, win_ref[:, 0:aw]), gq_ref) * Q_SCALE
    k = head_norm_rope(_dot(h, win_ref[:, aw:2 * aw]), gk_ref)
    v = _dot(h, win_ref[:, 2 * aw:3 * aw])

    k_ref[0] = k.astype(_BF16)
    for ib in range(nib):
        kmean_ref[0, 0, ib:ib + 1, :] = jnp.mean(
            k[ib * MOBA_BLOCK:(ib + 1) * MOBA_BLOCK], axis=0, keepdims=True)

    qT = q.T.astype(_BF16)
    vT = v.T.astype(_BF16)
    zeros = jnp.zeros((HEAD_DIM, MOBA_BLOCK), _BF16)
    pad_rows = lax.broadcasted_iota(jnp.int32, (V_ROWS - HEAD_DIM, MOBA_BLOCK), 0)
    ones_row = jnp.where(pad_rows == 0, 1.0, 0.0).astype(_BF16)
    for hd in range(N_HEADS):
        lo = (hd % HEADS_PER_VREG) * HEAD_DIM
        for ib in range(nib):
            cols = slice(ib * MOBA_BLOCK, (ib + 1) * MOBA_BLOCK)
            rows = slice(hd * HEAD_DIM, (hd + 1) * HEAD_DIM)
            qT_ref[0, hd, ib, lo:lo + HEAD_DIM, :] = qT[rows, cols]
            qT_ref[0, hd, ib, HEAD_DIM - lo:2 * HEAD_DIM - lo, :] = zeros
            vT_ref[0, hd, ib, 0:HEAD_DIM, :] = vT[rows, cols]
            vT_ref[0, hd, ib, HEAD_DIM:V_ROWS, :] = ones_row

    cw = CONV_WIDTH
    a = _dot(h, win_ref[:, 3 * aw:3 * aw + cw]) + glub_ref[:, 0:cw]
    g = _dot(h, win_ref[:, 3 * aw + cw:3 * aw + 2 * cw]) + glub_ref[:, cw:2 * cw]
    hg_ref[0] = a * jax.nn.sigmoid(g)


def _proj_call(x, g1, win, glub, gq, gk, gmat, cos2, sin2):
    B, S, D = x.shape
    tm = TM_PROJ
    nib = tm // MOBA_BLOCK
    nb = S // MOBA_BLOCK
    const = lambda b, t: (0, 0)
    return pl.pallas_call(
        _proj_kernel,
        grid=(B, S // tm),
        in_specs=[
            pl.BlockSpec((1, tm, D), lambda b, t: (b, t, 0)),
            pl.BlockSpec((1, D), const),
            pl.BlockSpec((D, D_IN), const, pipeline_mode=pl.Buffered(1)),
            pl.BlockSpec((1, 2 * CONV_WIDTH), const),
            pl.BlockSpec((1, ATTN_WIDTH), const),
            pl.BlockSpec((1, ATTN_WIDTH), const),
            pl.BlockSpec((ATTN_WIDTH, ATTN_WIDTH), const),
            pl.BlockSpec((tm, LANES), lambda b, t: (t, 0)),
            pl.BlockSpec((tm, LANES), lambda b, t: (t, 0)),
        ],
        out_specs=[
            pl.BlockSpec((1, N_HEADS, nib, LANES, MOBA_BLOCK), lambda b, t: (b, 0, t, 0, 0)),
            pl.BlockSpec((1, tm, ATTN_WIDTH), lambda b, t: (b, t, 0)),
            pl.BlockSpec((1, N_HEADS, nib, V_ROWS, MOBA_BLOCK), lambda b, t: (b, 0, t, 0, 0)),
            pl.BlockSpec((1, 1, nib, ATTN_WIDTH), lambda b, t: (b, t, 0, 0)),
            pl.BlockSpec((1, tm, CONV_WIDTH), lambda b, t: (b, t, 0)),
        ],
        out_shape=[
            jax.ShapeDtypeStruct((B, N_HEADS, nb, LANES, MOBA_BLOCK), _BF16),
            jax.ShapeDtypeStruct((B, S, ATTN_WIDTH), _BF16),
            jax.ShapeDtypeStruct((B, N_HEADS, nb, V_ROWS, MOBA_BLOCK), _BF16),
            jax.ShapeDtypeStruct((B, S // tm, nib, ATTN_WIDTH), _F32),
            jax.ShapeDtypeStruct((B, S, CONV_WIDTH), _F32),
        ],
        compiler_params=pltpu.CompilerParams(
            dimension_semantics=("arbitrary", "arbitrary"), vmem_limit_bytes=VMEM_LIMIT),
        name="moba_proj",
    )(x, g1, win, glub, gq, gk, gmat, cos2, sin2)


def _attn_kernel(qT_ref, k_ref, vT_ref, kmean_ref, o_ref,
                 acc_ref, m_ref, sel_ref, s_ref, smax_ref, p_ref):
    i = pl.program_id(1)
    nb = kmean_ref.shape[1]
    blk = MOBA_BLOCK
    row = lax.broadcasted_iota(jnp.int32, (blk, blk), 0)
    col = lax.broadcasted_iota(jnp.int32, (blk, blk), 1)
    causal = row <= col
    blk_id = lax.broadcasted_iota(jnp.int32, (nb, blk), 0).astype(_F32)

    def k_block(j, pair):
        start = pl.multiple_of(j * blk, blk)
        return k_ref[0, pl.ds(start, blk), pair * LANES:(pair + 1) * LANES]

    def stage(hd, j, masked):
        pair = hd // HEADS_PER_VREG
        s = _dot(k_block(j, pair), qT_ref[0, hd, 0])
        if masked:
            s = jnp.where(causal, s, NEG_BIG)
        s_ref[hd] = s
        smax_ref[hd] = jnp.max(s, axis=0, keepdims=True)

    def process(hd, t, j):
        on = sel_ref[hd, t] > 0.5
        m_old = m_ref[hd]
        m_new = jnp.maximum(m_old, jnp.where(on, smax_ref[hd], NEG_BIG))
        alpha = jnp.exp2(m_old - m_new)
        shift = jnp.broadcast_to(jnp.where(on, m_new, POS_BIG), (EXP_ROWS, blk))
        for r in range(blk // EXP_ROWS):
            rows = slice(r * EXP_ROWS, (r + 1) * EXP_ROWS)
            p_ref[hd, rows, :] = jnp.exp2(s_ref[hd, rows, :] - shift).astype(_BF16)
        acc_ref[hd] = alpha * acc_ref[hd] + _dot(vT_ref[0, hd, j], p_ref[hd])
        m_ref[hd] = m_new

    for hd in range(N_HEADS):
        pair = hd // HEADS_PER_VREG
        qh = qT_ref[0, hd, 0]
        km = kmean_ref[0, :, pair * LANES:(pair + 1) * LANES].astype(_BF16)
        avail = blk_id < i.astype(_F32)
        gate = jnp.where(avail, _dot(km, qh), NEG_BIG)
        sel = jnp.zeros((nb, blk), _F32)
        for _ in range(MOBA_TOPK):
            best = jnp.max(gate, axis=0, keepdims=True)
            lowest = jnp.min(jnp.where(gate == best, blk_id, float(nb)), axis=0, keepdims=True)
            take = (blk_id == lowest) & avail
            sel = jnp.where(take, 1.0, sel)
            avail = avail & jnp.logical_not(take)
            gate = jnp.where(take, NEG_BIG, gate)
        sel_ref[hd, 0] = jnp.ones((1, blk), _F32)
        for n in range(nb - 1):
            sel_ref[hd, n + 1] = sel[n:n + 1, :]
        m_ref[hd] = jnp.full((1, blk), NEG_BIG, _F32)
        acc_ref[hd] = jnp.zeros((V_ROWS, blk), _F32)
        stage(hd, i, masked=True)

    def sweep_step(u, carry):
        j_staged = jnp.where(u == 0, i, u - 1)
        for hd in range(N_HEADS):
            process(hd, u, j_staged)
            stage(hd, u, masked=False)
        return carry

    lax.fori_loop(0, i, sweep_step, 0)

    j_last = jnp.maximum(i - 1, 0)
    for hd in range(N_HEADS):
        process(hd, i, j_last)
        acc = acc_ref[hd]
        denom = acc[HEAD_DIM:HEAD_DIM + 1, :]
        o_ref[0, hd, 0] = (acc[0:HEAD_DIM, :] * (1.0 / denom)).astype(_BF16)


def _attn_call(qT, k, vT, kmean):
    B, H, nb, _, blk = qT.shape
    S = k.shape[1]
    return pl.pallas_call(
        _attn_kernel,
        grid=(B, nb),
        in_specs=[
            pl.BlockSpec((1, H, 1, LANES, blk), lambda b, i: (b, 0, i, 0, 0)),
            pl.BlockSpec((1, S, ATTN_WIDTH), lambda b, i: (b, 0, 0)),
            pl.BlockSpec((1, H, nb, V_ROWS, blk), lambda b, i: (b, 0, 0, 0, 0)),
            pl.BlockSpec((1, nb, ATTN_WIDTH), lambda b, i: (b, 0, 0)),
        ],
        out_specs=pl.BlockSpec((1, H, 1, HEAD_DIM, blk), lambda b, i: (b, 0, i, 0, 0)),
        out_shape=jax.ShapeDtypeStruct((B, H, nb, HEAD_DIM, blk), _BF16),
        scratch_shapes=[
            pltpu.VMEM((H, V_ROWS, blk), _F32),
            pltpu.VMEM((H, 1, blk), _F32),
            pltpu.VMEM((H, nb, 1, blk), _F32),
            pltpu.VMEM((H, blk, blk), _F32),
            pltpu.VMEM((H, 1, blk), _F32),
            pltpu.VMEM((H, blk, blk), _BF16),
        ],
        compiler_params=pltpu.CompilerParams(
            dimension_semantics=("arbitrary", "arbitrary"), vmem_limit_bytes=VMEM_LIMIT),
        name="moba_attn",
    )(qT, k, vT, kmean)


def _out_kernel(x_ref, oT_ref, hg0_ref, hgn_ref, halon_ref, dww_ref, dwb_ref, lng_ref, lnb_ref, wout_ref,
                g2_ref, wg_ref, wu_ref, wd_ref, y_ref, hbuf_ref, hshift_ref, wtap_ref, cact_ref,
                *, tiles_per_seq):
    tm = x_ref.shape[1]
    nib = tm // MOBA_BLOCK
    n = pl.program_id(0)

    n_chunks = tm // CONV_ROWS

    def exact_zero(v):
        return jnp.minimum(jnp.abs(v[0:SUBLANES, 0:LANES]), 0.0)

    def add_to_corner(v, z):
        top = jnp.concatenate([v[0:SUBLANES, 0:LANES] + z, v[0:SUBLANES, LANES:]], axis=1)
        return top if v.shape[0] == SUBLANES else jnp.concatenate([top, v[SUBLANES:]], axis=0)

    def conv_setup(hg_tile, halo):
        hbuf_ref[0:CONV_HALO, :] = halo
        hbuf_ref[CONV_HALO:CONV_HALO + tm, :] = hg_tile
        span = hshift_ref.shape[1]
        for r in range(1, SUBLANES):
            hshift_ref[r - 1] = hbuf_ref[r:r + span, :]
        for kk in range(CONV_KERNEL):
            wtap_ref[kk] = jnp.broadcast_to(dww_ref[kk:kk + 1, :], (SUBLANES, CONV_WIDTH))

    def conv_chunk(ci, after=None):
        c0 = ci * CONV_ROWS
        first = CONV_HALO - (CONV_KERNEL - 1)
        accs = [None] * (CONV_ROWS // SUBLANES)
        for kk in range(CONV_KERNEL):
            r = (first + kk) % SUBLANES
            lo = c0 + first + kk - r
            src = hbuf_ref if r == 0 else hshift_ref.at[r - 1]
            wk = wtap_ref[kk]
            for gi in range(len(accs)):
                term = wk * src[lo + gi * SUBLANES:lo + (gi + 1) * SUBLANES, :]
                if accs[gi] is None:
                    accs[gi] = term if after is None else add_to_corner(term, after)
                else:
                    accs[gi] = accs[gi] + term
        conv = jnp.concatenate(accs, axis=0) + dwb_ref[...]
        mu = jnp.mean(conv, axis=-1, keepdims=True)
        cen = conv - mu
        var = jnp.mean(cen * cen, axis=-1, keepdims=True)
        cn = cen * lax.rsqrt(var + EPS) * lng_ref[...] + lnb_ref[...]
        cact_ref[c0:c0 + CONV_ROWS, :] = (cn * jax.nn.sigmoid(cn)).astype(_BF16)
        return exact_zero(cn)

    @pl.when(n == 0)
    def _first_tile():
        conv_setup(hg0_ref[0], jnp.zeros((CONV_HALO, CONV_WIDTH), _F32))
        for ci in range(n_chunks):
            conv_chunk(ci)

    cact = cact_ref[...]
    nxt = jnp.minimum(n + 1, pl.num_programs(0) - 1)
    halo = halon_ref[0]
    conv_setup(hgn_ref[0], jnp.where(nxt % tiles_per_seq == 0, jnp.zeros_like(halo), halo))

    finished = []

    def dot_after_chunk(a, b):
        d = len(finished)
        out = _dot(a, b)
        if d < n_chunks:
            out = add_to_corner(out, conv_chunk(d, finished[d - CONV_CHAINS] if d >= CONV_CHAINS else None))
        finished.append(exact_zero(out))
        return out

    mix = _dot(cact, wout_ref[ATTN_WIDTH:ATTN_WIDTH + CONV_WIDTH, :])
    attn_parts = []
    for ib in range(nib):
        oT = oT_ref[0, :, ib].reshape(ATTN_WIDTH, MOBA_BLOCK)
        attn_parts.append(lax.dot_general(oT, wout_ref[0:ATTN_WIDTH, :], (((0,), (0,)), ((), ())),
                                          preferred_element_type=_F32))
    x1 = x_ref[0] + (mix + jnp.concatenate(attn_parts, axis=0))

    ms = jnp.mean(x1 * x1, axis=-1, keepdims=True)
    h2 = (x1 * lax.rsqrt(ms + EPS) * g2_ref[...]).astype(_BF16)
    y = x1
    for c0 in range(0, D_FF, FF_CHUNK):
        cs = slice(c0, min(c0 + FF_CHUNK, D_FF))
        gt = dot_after_chunk(h2, wg_ref[:, cs])
        up = dot_after_chunk(h2, wu_ref[:, cs])
        act = (gt * jax.nn.sigmoid(gt) * up).astype(_BF16)
        y = y + dot_after_chunk(act, wd_ref[cs, :])
    assert len(finished) >= n_chunks, "every conv chunk must be tied to a matmul"
    y_ref[0] = y


def _out_call(x, oT, hg, dww, dwb, lng, lnb, wout, g2, wg, wu, wd):
    B, S, D = x.shape
    tm = TM_OUT
    nib = tm // MOBA_BLOCK
    H = N_HEADS
    const = lambda n: (0, 0)
    resident = functools.partial(pl.BlockSpec, pipeline_mode=pl.Buffered(1))
    halo_blocks = tm // CONV_HALO
    nt = S // tm
    n_steps = B * nt

    def next_tile(n):
        nxt = jnp.minimum(n + 1, n_steps - 1)
        return nxt // nt, nxt % nt

    def next_hg(n):
        b, t = next_tile(n)
        return b, t, 0

    def next_halo(n):
        b, t = next_tile(n)
        return b, jnp.maximum(t * halo_blocks - 1, 0), 0

    return pl.pallas_call(
        functools.partial(_out_kernel, tiles_per_seq=nt),
        grid=(n_steps,),
        in_specs=[
            pl.BlockSpec((1, tm, D), lambda n: (n // nt, n % nt, 0)),
            pl.BlockSpec((1, H, nib, HEAD_DIM, MOBA_BLOCK), lambda n: (n // nt, 0, n % nt, 0, 0)),
            pl.BlockSpec((1, tm, CONV_WIDTH), lambda n: (0, 0, 0)),
            pl.BlockSpec((1, tm, CONV_WIDTH), next_hg),
            pl.BlockSpec((1, CONV_HALO, CONV_WIDTH), next_halo),
            pl.BlockSpec((CONV_KERNEL, CONV_WIDTH), const),
            pl.BlockSpec((1, CONV_WIDTH), const),
            pl.BlockSpec((1, CONV_WIDTH), const),
            pl.BlockSpec((1, CONV_WIDTH), const),
            resident((D, D), const),
            pl.BlockSpec((1, D), const),
            resident((D, D_FF), const),
            resident((D, D_FF), const),
            resident((D_FF, D), const),
        ],
        out_specs=pl.BlockSpec((1, tm, D), lambda n: (n // nt, n % nt, 0)),
        out_shape=jax.ShapeDtypeStruct((B, S, D), _F32),
        scratch_shapes=[
            pltpu.VMEM((CONV_HALO + tm, CONV_WIDTH), _F32),
            pltpu.VMEM((SUBLANES - 1, CONV_HALO + tm - SUBLANES, CONV_WIDTH), _F32),
            pltpu.VMEM((CONV_KERNEL, SUBLANES, CONV_WIDTH), _F32),
            pltpu.VMEM((tm, CONV_WIDTH), _BF16),
        ],
        compiler_params=pltpu.CompilerParams(
            dimension_semantics=("arbitrary",), vmem_limit_bytes=VMEM_LIMIT),
        name="moba_out_ffn",
    )(x, oT, hg, hg, hg, dww, dwb, lng, lnb, wout, g2, wg, wu, wd)


def _rope_tables(seq_len):
    pos = jnp.arange(seq_len, dtype=_F32)
    inv_freq = ROPE_THETA ** (-jnp.arange(0, HEAD_DIM, 2, dtype=_F32) / HEAD_DIM)
    ang = pos[:, None] * inv_freq[None, :]
    ang = jnp.concatenate([ang, ang], axis=-1)
    sign = jnp.where(jnp.arange(HEAD_DIM) < HEAD_DIM // 2, -1.0, 1.0).astype(_F32)
    cos2 = jnp.tile(jnp.cos(ang), (1, HEADS_PER_VREG))
    sin2 = jnp.tile(jnp.sin(ang) * sign[None, :], (1, HEADS_PER_VREG))
    return cos2, sin2


def _layer(x, norm1_g, w_in, glu_b, q_norm_g, k_norm_g, dw_w, dw_b, conv_ln_g, conv_ln_b,
           w_out, norm2_g, w_gate, w_up, w_down, cos2, sin2, gmat):
    B, S, _ = x.shape
    row = lambda a: a.reshape(1, -1)
    qT, k, vT, kmean, hg = _proj_call(
        x, row(norm1_g), w_in.astype(_BF16), row(glu_b),
        row(jnp.tile(q_norm_g, N_HEADS)), row(jnp.tile(k_norm_g, N_HEADS)), gmat, cos2, sin2)
    kmean = kmean.reshape(B, S // MOBA_BLOCK, ATTN_WIDTH)
    oT = _attn_call(qT, k, vT, kmean)
    return _out_call(x, oT, hg, dw_w, row(dw_b), row(conv_ln_g), row(conv_ln_b),
                     w_out.astype(_BF16), row(norm2_g), w_gate.astype(_BF16),
                     w_up.astype(_BF16), w_down.astype(_BF16))


def kernel(x, norm1_g, w_in, glu_b, q_norm_g, k_norm_g, dw_w, dw_b, conv_ln_g, conv_ln_b, w_out,
           norm2_g, w_gate, w_up, w_down):
    S = x.shape[1]
    cos2, sin2 = _rope_tables(S)
    head_of = jnp.arange(ATTN_WIDTH) // HEAD_DIM
    gmat = jnp.where(head_of[:, None] == head_of[None, :], 1.0 / HEAD_DIM, 0.0).astype(_BF16)
    for l in range(norm1_g.shape[0]):
        x = _layer(x, norm1_g[l], w_in[l], glu_b[l], q_norm_g[l], k_norm_g[l], dw_w[l], dw_b[l],
                   conv_ln_g[l], conv_ln_b[l], w_out[l], norm2_g[l], w_gate[l], w_up[l], w_down[l],
                   cos2, sin2, gmat)
    return x
```

```python
import functools

import jax
import jax.numpy as jnp
from jax import lax
from jax.experimental import pallas as pl
from jax.experimental.pallas import tpu as pltpu

D_MODEL = 1024
ATTN_WIDTH = 512
CONV_WIDTH = 512
N_HEADS = 8
HEAD_DIM = 64
CONV_KERNEL = 31
MOBA_BLOCK = 256
MOBA_TOPK = 3
ROPE_THETA = 10000.0
D_FF = 2816
EPS = 1e-6
D_IN = 3 * ATTN_WIDTH + 2 * CONV_WIDTH

LANES = 128
HEADS_PER_VREG = LANES // HEAD_DIM
V_ROWS = 80
Q_BLOCKS = 2
EXP_ROWS = 16
CONV_HALO = 32
LOG2_E = 1.4426950408889634
Q_SCALE = HEAD_DIM ** -0.5 * LOG2_E
NEG_BIG = -1e30
POS_BIG = 1e30

TM_PROJ = 512
TM_OUT = 512
FF_CHUNK = 256
SUBLANES = 8
CONV_ROWS = 16
CONV_CHAINS = 4
VMEM_LIMIT = 56 * 1024 * 1024

_BF16 = jnp.bfloat16
_F32 = jnp.float32


def _dot(a, b):
    return jnp.dot(a, b, preferred_element_type=_F32)


def _proj_kernel(x_ref, g1_ref, win_ref, glub_ref, gq_ref, gk_ref, gmat_ref, cos_ref, sin_ref,
                 qT_ref, k_ref, vT_ref, kmean_ref, hg_ref):
    tm = x_ref.shape[1]
    nib = tm // MOBA_BLOCK
    x = x_ref[0]
    ms = jnp.mean(x * x, axis=-1, keepdims=True)
    h = (x * lax.rsqrt(ms + EPS) * g1_ref[...]).astype(_BF16)

    cos = cos_ref[...]
    sin = sin_ref[...]
    lane = lax.broadcasted_iota(jnp.int32, (tm, LANES), 1)
    first_half = (lane & (HEAD_DIM // 2)) == 0

    def head_norm_rope(p, g_ref):
        msq = _dot((p * p).astype(_BF16), gmat_ref[...])
        pn = p * lax.rsqrt(msq + EPS) * g_ref[...]
        outs = []
        for c in range(ATTN_WIDTH // LANES):
            xc = pn[:, c * LANES:(c + 1) * LANES]
            partner = jnp.where(first_half,
                                pltpu.roll(xc, LANES - HEAD_DIM // 2, 1),
                                pltpu.roll(xc, HEAD_DIM // 2, 1))
            outs.append(xc * cos + partner * sin)
        return jnp.concatenate(outs, axis=1)

    aw = ATTN_WIDTH
    q = head_norm_rope(_dot(h, win_ref[:, 0:aw]), gq_ref) * Q_SCALE
    k = head_norm_rope(_dot(h, win_ref[:, aw:2 * aw]), gk_ref)
    v = _dot(h, win_ref[:, 2 * aw:3 * aw])

    k_ref[0] = k.astype(_BF16)
    for ib in range(nib):
        kmean_ref[0, 0, ib:ib + 1, :] = jnp.mean(
            k[ib * MOBA_BLOCK:(ib + 1) * MOBA_BLOCK], axis=0, keepdims=True)

    qT = q.T.astype(_BF16)
    vT = v.T.astype(_BF16)
    zeros = jnp.zeros((HEAD_DIM, MOBA_BLOCK), _BF16)
    pad_rows = lax.broadcasted_iota(jnp.int32, (V_ROWS - HEAD_DIM, MOBA_BLOCK), 0)
    ones_row = jnp.where(pad_rows == 0, 1.0, 0.0).astype(_BF16)
    for hd in range(N_HEADS):
        lo = (hd % HEADS_PER_VREG) * HEAD_DIM
        for ib in range(nib):
            cols = slice(ib * MOBA_BLOCK, (ib + 1) * MOBA_BLOCK)
            rows = slice(hd * HEAD_DIM, (hd + 1) * HEAD_DIM)
            qT_ref[0, hd, ib, lo:lo + HEAD_DIM, :] = qT[rows, cols]
            qT_ref[0, hd, ib, HEAD_DIM - lo:2 * HEAD_DIM - lo, :] = zeros
            vT_ref[0, hd, ib, 0:HEAD_DIM, :] = vT[rows, cols]
            vT_ref[0, hd, ib, HEAD_DIM:V_ROWS, :] = ones_row

    cw = CONV_WIDTH
    a = _dot(h, win_ref[:, 3 * aw:3 * aw + cw]) + glub_ref[:, 0:cw]
    g = _dot(h, win_ref[:, 3 * aw + cw:3 * aw + 2 * cw]) + glub_ref[:, cw:2 * cw]
    hg_ref[0] = a * jax.nn.sigmoid(g)


def _proj_call(x, g1, win, glub, gq, gk, gmat, cos2, sin2):
    B, S, D = x.shape
    tm = TM_PROJ
    nib = tm // MOBA_BLOCK
    nb = S // MOBA_BLOCK
    const = lambda b, t: (0, 0)
    return pl.pallas_call(
        _proj_kernel,
        grid=(B, S // tm),
        in_specs=[
            pl.BlockSpec((1, tm, D), lambda b, t: (b, t, 0)),
            pl.BlockSpec((1, D), const),
            pl.BlockSpec((D, D_IN), const, pipeline_mode=pl.Buffered(1)),
            pl.BlockSpec((1, 2 * CONV_WIDTH), const),
            pl.BlockSpec((1, ATTN_WIDTH), const),
            pl.BlockSpec((1, ATTN_WIDTH), const),
            pl.BlockSpec((ATTN_WIDTH, ATTN_WIDTH), const),
            pl.BlockSpec((tm, LANES), lambda b, t: (t, 0)),
            pl.BlockSpec((tm, LANES), lambda b, t: (t, 0)),
        ],
        out_specs=[
            pl.BlockSpec((1, N_HEADS, nib, LANES, MOBA_BLOCK), lambda b, t: (b, 0, t, 0, 0)),
            pl.BlockSpec((1, tm, ATTN_WIDTH), lambda b, t: (b, t, 0)),
            pl.BlockSpec((1, N_HEADS, nib, V_ROWS, MOBA_BLOCK), lambda b, t: (b, 0, t, 0, 0)),
            pl.BlockSpec((1, 1, nib, ATTN_WIDTH), lambda b, t: (b, t, 0, 0)),
            pl.BlockSpec((1, tm, CONV_WIDTH), lambda b, t: (b, t, 0)),
        ],
        out_shape=[
            jax.ShapeDtypeStruct((B, N_HEADS, nb, LANES, MOBA_BLOCK), _BF16),
            jax.ShapeDtypeStruct((B, S, ATTN_WIDTH), _BF16),
            jax.ShapeDtypeStruct((B, N_HEADS, nb, V_ROWS, MOBA_BLOCK), _BF16),
            jax.ShapeDtypeStruct((B, S // tm, nib, ATTN_WIDTH), _F32),
            jax.ShapeDtypeStruct((B, S, CONV_WIDTH), _F32),
        ],
        compiler_params=pltpu.CompilerParams(
            dimension_semantics=("arbitrary", "arbitrary"), vmem_limit_bytes=VMEM_LIMIT),
        name="moba_proj",
    )(x, g1, win, glub, gq, gk, gmat, cos2, sin2)


def _attn_kernel(qT_ref, k_ref, vT_ref, kmean_ref, o_ref,
                 acc_ref, m_ref, sel_ref, s_ref, smax_ref, p_ref):
    assert Q_BLOCKS == 2
    i0 = pl.program_id(1) * Q_BLOCKS
    nb = kmean_ref.shape[1]
    blk = MOBA_BLOCK
    own = nb
    row = lax.broadcasted_iota(jnp.int32, (blk, blk), 0)
    col = lax.broadcasted_iota(jnp.int32, (blk, blk), 1)
    causal = row <= col
    blk_id = lax.broadcasted_iota(jnp.int32, (nb, blk), 0).astype(_F32)
    chains = [(qb, hd) for qb in range(Q_BLOCKS) for hd in range(N_HEADS)]

    def k_block(j, pair):
        start = pl.multiple_of(j * blk, blk)
        return k_ref[0, pl.ds(start, blk), pair * LANES:(pair + 1) * LANES]

    def stage(c, j, masked):
        qb, hd = chains[c]
        s = _dot(k_block(j, hd // HEADS_PER_VREG), qT_ref[0, hd, qb])
        if masked:
            s = jnp.where(causal, s, NEG_BIG)
        s_ref[c] = s
        smax_ref[c] = jnp.max(s, axis=0, keepdims=True)

    def process(c, sel_row, j):
        qb, hd = chains[c]
        on = sel_ref[c, sel_row] > 0.5
        m_old = m_ref[c]
        m_new = jnp.maximum(m_old, jnp.where(on, smax_ref[c], NEG_BIG))
        alpha = jnp.exp2(m_old - m_new)
        shift = jnp.broadcast_to(jnp.where(on, m_new, POS_BIG), (EXP_ROWS, blk))
        for r in range(blk // EXP_ROWS):
            rows = slice(r * EXP_ROWS, (r + 1) * EXP_ROWS)
            p_ref[c, rows, :] = jnp.exp2(s_ref[c, rows, :] - shift).astype(_BF16)
        acc_ref[c] = alpha * acc_ref[c] + _dot(vT_ref[0, hd, j], p_ref[c])
        m_ref[c] = m_new

    for c, (qb, hd) in enumerate(chains):
        pair = hd // HEADS_PER_VREG
        qh = qT_ref[0, hd, qb]
        km = kmean_ref[0, :, pair * LANES:(pair + 1) * LANES].astype(_BF16)
        avail = blk_id < (i0 + qb).astype(_F32)
        gate = jnp.where(avail, _dot(km, qh), NEG_BIG)
        sel = jnp.zeros((nb, blk), _F32)
        for _ in range(MOBA_TOPK):
            best = jnp.max(gate, axis=0, keepdims=True)
            lowest = jnp.min(jnp.where(gate == best, blk_id, float(nb)), axis=0, keepdims=True)
            take = (blk_id == lowest) & avail
            sel = jnp.where(take, 1.0, sel)
            avail = avail & jnp.logical_not(take)
            gate = jnp.where(take, NEG_BIG, gate)
        for n in range(nb):
            sel_ref[c, n] = sel[n:n + 1, :]
        sel_ref[c, own] = jnp.ones((1, blk), _F32)
        m_ref[c] = jnp.full((1, blk), NEG_BIG, _F32)
        acc_ref[c] = jnp.zeros((V_ROWS, blk), _F32)
        stage(c, i0 + qb, masked=True)

    def sweep_step(u, carry):
        for c, (qb, hd) in enumerate(chains):
            process(c, jnp.where(u == 0, own, u - 1), jnp.where(u == 0, i0 + qb, u - 1))
            stage(c, u, masked=False)
        return carry

    lax.fori_loop(0, i0, sweep_step, 0)

    for c, (qb, hd) in enumerate(chains):
        process(c, jnp.where(i0 == 0, own, i0 - 1), jnp.where(i0 == 0, i0 + qb, i0 - 1))
        if qb == 1:
            stage(c, i0, masked=False)

    @pl.when(i0 >= 0)
    def _second_query_block_tail():
        for c, (qb, hd) in enumerate(chains):
            if qb == 1:
                process(c, i0, i0)

    for c, (qb, hd) in enumerate(chains):
        acc = acc_ref[c]
        denom = acc[HEAD_DIM:HEAD_DIM + 1, :]
        o_ref[0, hd, qb] = (acc[0:HEAD_DIM, :] * (1.0 / denom)).astype(_BF16)


def _attn_call(qT, k, vT, kmean):
    B, H, nb, _, blk = qT.shape
    S = k.shape[1]
    n_chains = Q_BLOCKS * H
    return pl.pallas_call(
        _attn_kernel,
        grid=(B, nb // Q_BLOCKS),
        in_specs=[
            pl.BlockSpec((1, H, Q_BLOCKS, LANES, blk), lambda b, g: (b, 0, g, 0, 0)),
            pl.BlockSpec((1, S, ATTN_WIDTH), lambda b, g: (b, 0, 0)),
            pl.BlockSpec((1, H, nb, V_ROWS, blk), lambda b, g: (b, 0, 0, 0, 0)),
            pl.BlockSpec((1, nb, ATTN_WIDTH), lambda b, g: (b, 0, 0)),
        ],
        out_specs=pl.BlockSpec((1, H, Q_BLOCKS, HEAD_DIM, blk), lambda b, g: (b, 0, g, 0, 0)),
        out_shape=jax.ShapeDtypeStruct((B, H, nb, HEAD_DIM, blk), _BF16),
        scratch_shapes=[
            pltpu.VMEM((n_chains, V_ROWS, blk), _F32),
            pltpu.VMEM((n_chains, 1, blk), _F32),
            pltpu.VMEM((n_chains, nb + 1, 1, blk), _F32),
            pltpu.VMEM((n_chains, blk, blk), _F32),
            pltpu.VMEM((n_chains, 1, blk), _F32),
            pltpu.VMEM((n_chains, blk, blk), _BF16),
        ],
        compiler_params=pltpu.CompilerParams(
            dimension_semantics=("arbitrary", "arbitrary"), vmem_limit_bytes=VMEM_LIMIT),
        name="moba_attn",
    )(qT, k, vT, kmean)


def _out_kernel(x_ref, oT_ref, hg0_ref, hgn_ref, halon_ref, dww_ref, dwb_ref, lng_ref, lnb_ref, wout_ref,
                g2_ref, wg_ref, wu_ref, wd_ref, y_ref, hbuf_ref, hshift_ref, wtap_ref, cact_ref,
                *, tiles_per_seq):
    tm = x_ref.shape[1]
    nib = tm // MOBA_BLOCK
    n = pl.program_id(0)

    n_chunks = tm // CONV_ROWS

    def exact_zero(v):
        return jnp.minimum(jnp.abs(v[0:SUBLANES, 0:LANES]), 0.0)

    def add_to_corner(v, z):
        top = jnp.concatenate([v[0:SUBLANES, 0:LANES] + z, v[0:SUBLANES, LANES:]], axis=1)
        return top if v.shape[0] == SUBLANES else jnp.concatenate([top, v[SUBLANES:]], axis=0)

    def conv_setup(hg_tile, halo):
        hbuf_ref[0:CONV_HALO, :] = halo
        hbuf_ref[CONV_HALO:CONV_HALO + tm, :] = hg_tile
        span = hshift_ref.shape[1]
        for r in range(1, SUBLANES):
            hshift_ref[r - 1] = hbuf_ref[r:r + span, :]
        for kk in range(CONV_KERNEL):
            wtap_ref[kk] = jnp.broadcast_to(dww_ref[kk:kk + 1, :], (SUBLANES, CONV_WIDTH))

    def conv_chunk(ci, after=None):
        c0 = ci * CONV_ROWS
        first = CONV_HALO - (CONV_KERNEL - 1)
        accs = [None] * (CONV_ROWS // SUBLANES)
        for kk in range(CONV_KERNEL):
            r = (first + kk) % SUBLANES
            lo = c0 + first + kk - r
            src = hbuf_ref if r == 0 else hshift_ref.at[r - 1]
            wk = wtap_ref[kk]
            for gi in range(len(accs)):
                term = wk * src[lo + gi * SUBLANES:lo + (gi + 1) * SUBLANES, :]
                if accs[gi] is None:
                    accs[gi] = term if after is None else add_to_corner(term, after)
                else:
                    accs[gi] = accs[gi] + term
        conv = jnp.concatenate(accs, axis=0) + dwb_ref[...]
        mu = jnp.mean(conv, axis=-1, keepdims=True)
        cen = conv - mu
        var = jnp.mean(cen * cen, axis=-1, keepdims=True)
        cn = cen * lax.rsqrt(var + EPS) * lng_ref[...] + lnb_ref[...]
        cact_ref[c0:c0 + CONV_ROWS, :] = (cn * jax.nn.sigmoid(cn)).astype(_BF16)
        return exact_zero(cn)

    @pl.when(n == 0)
    def _first_tile():
        conv_setup(hg0_ref[0], jnp.zeros((CONV_HALO, CONV_WIDTH), _F32))
        for ci in range(n_chunks):
            conv_chunk(ci)

    cact = cact_ref[...]
    nxt = jnp.minimum(n + 1, pl.num_programs(0) - 1)
    halo = halon_ref[0]
    conv_setup(hgn_ref[0], jnp.where(nxt % tiles_per_seq == 0, jnp.zeros_like(halo), halo))

    finished = []

    def dot_after_chunk(a, b):
        d = len(finished)
        out = _dot(a, b)
        if d < n_chunks:
            out = add_to_corner(out, conv_chunk(d, finished[d - CONV_CHAINS] if d >= CONV_CHAINS else None))
        finished.append(exact_zero(out))
        return out

    mix = _dot(cact, wout_ref[ATTN_WIDTH:ATTN_WIDTH + CONV_WIDTH, :])
    attn_parts = []
    for ib in range(nib):
        oT = oT_ref[0, :, ib].reshape(ATTN_WIDTH, MOBA_BLOCK)
        attn_parts.append(lax.dot_general(oT, wout_ref[0:ATTN_WIDTH, :], (((0,), (0,)), ((), ())),
                                          preferred_element_type=_F32))
    x1 = x_ref[0] + (mix + jnp.concatenate(attn_parts, axis=0))

    ms = jnp.mean(x1 * x1, axis=-1, keepdims=True)
    h2 = (x1 * lax.rsqrt(ms + EPS) * g2_ref[...]).astype(_BF16)
    y = x1
    for c0 in range(0, D_FF, FF_CHUNK):
        cs = slice(c0, min(c0 + FF_CHUNK, D_FF))
        gt = dot_after_chunk(h2, wg_ref[:, cs])
        up = dot_after_chunk(h2, wu_ref[:, cs])
        act = (gt * jax.nn.sigmoid(gt) * up).astype(_BF16)
        y = y + dot_after_chunk(act, wd_ref[cs, :])
    assert len(finished) >= n_chunks, "every conv chunk must be tied to a matmul"
    y_ref[0] = y


def _out_call(x, oT, hg, dww, dwb, lng, lnb, wout, g2, wg, wu, wd):
    B, S, D = x.shape
    tm = TM_OUT
    nib = tm // MOBA_BLOCK
    H = N_HEADS
    const = lambda n: (0, 0)
    resident = functools.partial(pl.BlockSpec, pipeline_mode=pl.Buffered(1))
    halo_blocks = tm // CONV_HALO
    nt = S // tm
    n_steps = B * nt

    def next_tile(n):
        nxt = jnp.minimum(n + 1, n_steps - 1)
        return nxt // nt, nxt % nt

    def next_hg(n):
        b, t = next_tile(n)
        return b, t, 0

    def next_halo(n):
        b, t = next_tile(n)
        return b, jnp.maximum(t * halo_blocks - 1, 0), 0

    return pl.pallas_call(
        functools.partial(_out_kernel, tiles_per_seq=nt),
        grid=(n_steps,),
        in_specs=[
            pl.BlockSpec((1, tm, D), lambda n: (n // nt, n % nt, 0)),
            pl.BlockSpec((1, H, nib, HEAD_DIM, MOBA_BLOCK), lambda n: (n // nt, 0, n % nt, 0, 0)),
            pl.BlockSpec((1, tm, CONV_WIDTH), lambda n: (0, 0, 0)),
            pl.BlockSpec((1, tm, CONV_WIDTH), next_hg),
            pl.BlockSpec((1, CONV_HALO, CONV_WIDTH), next_halo),
            pl.BlockSpec((CONV_KERNEL, CONV_WIDTH), const),
            pl.BlockSpec((1, CONV_WIDTH), const),
            pl.BlockSpec((1, CONV_WIDTH), const),
            pl.BlockSpec((1, CONV_WIDTH), const),
            resident((D, D), const),
            pl.BlockSpec((1, D), const),
            resident((D, D_FF), const),
            resident((D, D_FF), const),
            resident((D_FF, D), const),
        ],
        out_specs=pl.BlockSpec((1, tm, D), lambda n: (n // nt, n % nt, 0)),
        out_shape=jax.ShapeDtypeStruct((B, S, D), _F32),
        scratch_shapes=[
            pltpu.VMEM((CONV_HALO + tm, CONV_WIDTH), _F32),
            pltpu.VMEM((SUBLANES - 1, CONV_HALO + tm - SUBLANES, CONV_WIDTH), _F32),
            pltpu.VMEM((CONV_KERNEL, SUBLANES, CONV_WIDTH), _F32),
            pltpu.VMEM((tm, CONV_WIDTH), _BF16),
        ],
        compiler_params=pltpu.CompilerParams(
            dimension_semantics=("arbitrary",), vmem_limit_bytes=VMEM_LIMIT),
        name="moba_out_ffn",
    )(x, oT, hg, hg, hg, dww, dwb, lng, lnb, wout, g2, wg, wu, wd)


def _rope_tables(seq_len):
    pos = jnp.arange(seq_len, dtype=_F32)
    inv_freq = ROPE_THETA ** (-jnp.arange(0, HEAD_DIM, 2, dtype=_F32) / HEAD_DIM)
    ang = pos[:, None] * inv_freq[None, :]
    ang = jnp.concatenate([ang, ang], axis=-1)
    sign = jnp.where(jnp.arange(HEAD_DIM) < HEAD_DIM // 2, -1.0, 1.0).astype(_F32)
    cos2 = jnp.tile(jnp.cos(ang), (1, HEADS_PER_VREG))
    sin2 = jnp.tile(jnp.sin(ang) * sign[None, :], (1, HEADS_PER_VREG))
    return cos2, sin2


def _layer(x, norm1_g, w_in, glu_b, q_norm_g, k_norm_g, dw_w, dw_b, conv_ln_g, conv_ln_b,
           w_out, norm2_g, w_gate, w_up, w_down, cos2, sin2, gmat):
    B, S, _ = x.shape
    row = lambda a: a.reshape(1, -1)
    qT, k, vT, kmean, hg = _proj_call(
        x, row(norm1_g), w_in.astype(_BF16), row(glu_b),
        row(jnp.tile(q_norm_g, N_HEADS)), row(jnp.tile(k_norm_g, N_HEADS)), gmat, cos2, sin2)
    kmean = kmean.reshape(B, S // MOBA_BLOCK, ATTN_WIDTH)
    oT = _attn_call(qT, k, vT, kmean)
    return _out_call(x, oT, hg, dw_w, row(dw_b), row(conv_ln_g), row(conv_ln_b),
                     w_out.astype(_BF16), row(norm2_g), w_gate.astype(_BF16),
                     w_up.astype(_BF16), w_down.astype(_BF16))


def kernel(x, norm1_g, w_in, glu_b, q_norm_g, k_norm_g, dw_w, dw_b, conv_ln_g, conv_ln_b, w_out,
           norm2_g, w_gate, w_up, w_down):
    S = x.shape[1]
    cos2, sin2 = _rope_tables(S)
    head_of = jnp.arange(ATTN_WIDTH) // HEAD_DIM
    gmat = jnp.where(head_of[:, None] == head_of[None, :], 1.0 / HEAD_DIM, 0.0).astype(_BF16)
    for l in range(norm1_g.shape[0]):
        x = _layer(x, norm1_g[l], w_in[l], glu_b[l], q_norm_g[l], k_norm_g[l], dw_w[l], dw_b[l],
                   conv_ln_g[l], conv_ln_b[l], w_out[l], norm2_g[l], w_gate[l], w_up[l], w_down[l],
                   cos2, sin2, gmat)
    return x
```

```python
import functools

import jax
import jax.numpy as jnp
from jax import lax
from jax.experimental import pallas as pl
from jax.experimental.pallas import tpu as pltpu

D_MODEL = 1024
ATTN_WIDTH = 512
CONV_WIDTH = 512
N_HEADS = 8
HEAD_DIM = 64
CONV_KERNEL = 31
MOBA_BLOCK = 256
MOBA_TOPK = 3
ROPE_THETA = 10000.0
D_FF = 2816
EPS = 1e-6
D_IN = 3 * ATTN_WIDTH + 2 * CONV_WIDTH

LANES = 128
HEADS_PER_VREG = LANES // HEAD_DIM
V_ROWS = 80
Q_BLOCKS = 2
EXP_ROWS = 16
CONV_HALO = 32
LOG2_E = 1.4426950408889634
Q_SCALE = HEAD_DIM ** -0.5 * LOG2_E
NEG_BIG = -1e30
POS_BIG = 1e30

TM_PROJ = 512
TM_OUT = 512
FF_CHUNK = 256
SUBLANES = 8
CONV_ROWS = 16
CONV_CHAINS = 4
VMEM_LIMIT = 56 * 1024 * 1024

_BF16 = jnp.bfloat16
_F32 = jnp.float32


def _dot(a, b):
    return jnp.dot(a, b, preferred_element_type=_F32)


def _proj_kernel(x_ref, g1_ref, win_ref, glub_ref, gq_ref, gk_ref, gmat_ref, cos_ref, sin_ref,
                 wout_f32, wgate_f32, wup_f32, wdown_f32,
                 qT_ref, k_ref, vT_ref, kmean_ref, hg_ref, wout_bf, wgate_bf, wup_bf, wdown_bf,
                 winbf_ref):
    tm = x_ref.shape[1]
    nib = tm // MOBA_BLOCK

    @pl.when((pl.program_id(0) == 0) & (pl.program_id(1) == 0))
    def _cast_w_in():
        for c0 in range(0, D_IN, ATTN_WIDTH):
            winbf_ref[:, c0:c0 + ATTN_WIDTH] = win_ref[:, c0:c0 + ATTN_WIDTH].astype(_BF16)

    for src, dst in ((wout_f32, wout_bf), (wgate_f32, wgate_bf), (wup_f32, wup_bf), (wdown_f32, wdown_bf)):
        dst[...] = src[...].astype(_BF16)
    win_ref = winbf_ref

    x = x_ref[0]
    ms = jnp.mean(x * x, axis=-1, keepdims=True)
    h = (x * lax.rsqrt(ms + EPS) * g1_ref[...]).astype(_BF16)

    cos = cos_ref[...]
    sin = sin_ref[...]
    lane = lax.broadcasted_iota(jnp.int32, (tm, LANES), 1)
    first_half = (lane & (HEAD_DIM // 2)) == 0

    def head_norm_rope(p, g_ref):
        msq = _dot((p * p).astype(_BF16), gmat_ref[...])
        pn = p * lax.rsqrt(msq + EPS) * g_ref[...]
        outs = []
        for c in range(ATTN_WIDTH // LANES):
            xc = pn[:, c * LANES:(c + 1) * LANES]
            partner = jnp.where(first_half,
                                pltpu.roll(xc, LANES - HEAD_DIM // 2, 1),
                                pltpu.roll(xc, HEAD_DIM // 2, 1))
            outs.append(xc * cos + partner * sin)
        return jnp.concatenate(outs, axis=1)

    aw = ATTN_WIDTH
    q = head_norm_rope(_dot(h, win_ref[:, 0:aw]), gq_ref) * Q_SCALE
    k = head_norm_rope(_dot(h, win_ref[:, aw:2 * aw]), gk_ref)
    v = _dot(h, win_ref[:, 2 * aw:3 * aw])

    k_ref[0] = k.astype(_BF16)
    for ib in range(nib):
        kmean_ref[0, 0, ib:ib + 1, :] = jnp.mean(
            k[ib * MOBA_BLOCK:(ib + 1) * MOBA_BLOCK], axis=0, keepdims=True)

    qT = q.T.astype(_BF16)
    vT = v.T.astype(_BF16)
    zeros = jnp.zeros((HEAD_DIM, MOBA_BLOCK), _BF16)
    pad_rows = lax.broadcasted_iota(jnp.int32, (V_ROWS - HEAD_DIM, MOBA_BLOCK), 0)
    ones_row = jnp.where(pad_rows == 0, 1.0, 0.0).astype(_BF16)
    for hd in range(N_HEADS):
        lo = (hd % HEADS_PER_VREG) * HEAD_DIM
        for ib in range(nib):
            cols = slice(ib * MOBA_BLOCK, (ib + 1) * MOBA_BLOCK)
            rows = slice(hd * HEAD_DIM, (hd + 1) * HEAD_DIM)
            qT_ref[0, hd, ib, lo:lo + HEAD_DIM, :] = qT[rows, cols]
            qT_ref[0, hd, ib, HEAD_DIM - lo:2 * HEAD_DIM - lo, :] = zeros
            vT_ref[0, hd, ib, 0:HEAD_DIM, :] = vT[rows, cols]
            vT_ref[0, hd, ib, HEAD_DIM:V_ROWS, :] = ones_row

    cw = CONV_WIDTH
    a = _dot(h, win_ref[:, 3 * aw:3 * aw + cw]) + glub_ref[:, 0:cw]
    g = _dot(h, win_ref[:, 3 * aw + cw:3 * aw + 2 * cw]) + glub_ref[:, cw:2 * cw]
    hg_ref[0] = a * jax.nn.sigmoid(g)


def _proj_call(x, g1, win, glub, gq, gk, gmat, cos2, sin2, later_weights):
    B, S, D = x.shape
    tm = TM_PROJ
    nib = tm // MOBA_BLOCK
    nb = S // MOBA_BLOCK
    nt = S // tm
    slab = D // (B * nt)
    assert slab * B * nt == D and slab % 16 == 0
    const = lambda b, t: (0, 0)
    slab_specs = [pl.BlockSpec((slab, w.shape[1]), lambda b, t: (b * nt + t, 0)) for w in later_weights]
    return pl.pallas_call(
        _proj_kernel,
        grid=(B, nt),
        in_specs=[
            pl.BlockSpec((1, tm, D), lambda b, t: (b, t, 0)),
            pl.BlockSpec((1, D), const),
            pl.BlockSpec((D, D_IN), const, pipeline_mode=pl.Buffered(1)),
            pl.BlockSpec((1, 2 * CONV_WIDTH), const),
            pl.BlockSpec((1, ATTN_WIDTH), const),
            pl.BlockSpec((1, ATTN_WIDTH), const),
            pl.BlockSpec((ATTN_WIDTH, ATTN_WIDTH), const),
            pl.BlockSpec((tm, LANES), lambda b, t: (t, 0)),
            pl.BlockSpec((tm, LANES), lambda b, t: (t, 0)),
        ] + slab_specs,
        out_specs=[
            pl.BlockSpec((1, N_HEADS, nib, LANES, MOBA_BLOCK), lambda b, t: (b, 0, t, 0, 0)),
            pl.BlockSpec((1, tm, ATTN_WIDTH), lambda b, t: (b, t, 0)),
            pl.BlockSpec((1, N_HEADS, nib, V_ROWS, MOBA_BLOCK), lambda b, t: (b, 0, t, 0, 0)),
            pl.BlockSpec((1, 1, nib, ATTN_WIDTH), lambda b, t: (b, t, 0, 0)),
            pl.BlockSpec((1, tm, CONV_WIDTH), lambda b, t: (b, t, 0)),
        ] + slab_specs,
        out_shape=[
            jax.ShapeDtypeStruct((B, N_HEADS, nb, LANES, MOBA_BLOCK), _BF16),
            jax.ShapeDtypeStruct((B, S, ATTN_WIDTH), _BF16),
            jax.ShapeDtypeStruct((B, N_HEADS, nb, V_ROWS, MOBA_BLOCK), _BF16),
            jax.ShapeDtypeStruct((B, S // tm, nib, ATTN_WIDTH), _F32),
            jax.ShapeDtypeStruct((B, S, CONV_WIDTH), _F32),
        ] + [jax.ShapeDtypeStruct(w.shape, _BF16) for w in later_weights],
        scratch_shapes=[pltpu.VMEM((D, D_IN), _BF16)],
        compiler_params=pltpu.CompilerParams(
            dimension_semantics=("arbitrary", "arbitrary"), vmem_limit_bytes=VMEM_LIMIT),
        name="moba_proj",
    )(x, g1, win, glub, gq, gk, gmat, cos2, sin2, *later_weights)


def _attn_kernel(qT_ref, k_ref, vT_ref, kmean_ref, o_ref,
                 acc_ref, m_ref, sel_ref, s_ref, smax_ref, p_ref):
    assert Q_BLOCKS == 2
    i0 = pl.program_id(1) * Q_BLOCKS
    nb = kmean_ref.shape[1]
    blk = MOBA_BLOCK
    own = nb
    row = lax.broadcasted_iota(jnp.int32, (blk, blk), 0)
    col = lax.broadcasted_iota(jnp.int32, (blk, blk), 1)
    causal = row <= col
    blk_id = lax.broadcasted_iota(jnp.int32, (nb, blk), 0).astype(_F32)
    chains = [(qb, hd) for qb in range(Q_BLOCKS) for hd in range(N_HEADS)]

    def k_block(j, pair):
        start = pl.multiple_of(j * blk, blk)
        return k_ref[0, pl.ds(start, blk), pair * LANES:(pair + 1) * LANES]

    def stage(c, j, masked):
        qb, hd = chains[c]
        s = _dot(k_block(j, hd // HEADS_PER_VREG), qT_ref[0, hd, qb])
        if masked:
            s = jnp.where(causal, s, NEG_BIG)
        s_ref[c] = s
        smax_ref[c] = jnp.max(s, axis=0, keepdims=True)

    def process(c, sel_row, j):
        qb, hd = chains[c]
        on = sel_ref[c, sel_row] > 0.5
        m_old = m_ref[c]
        m_new = jnp.maximum(m_old, jnp.where(on, smax_ref[c], NEG_BIG))
        alpha = jnp.exp2(m_old - m_new)
        shift = jnp.broadcast_to(jnp.where(on, m_new, POS_BIG), (EXP_ROWS, blk))
        for r in range(blk // EXP_ROWS):
            rows = slice(r * EXP_ROWS, (r + 1) * EXP_ROWS)
            p_ref[c, rows, :] = jnp.exp2(s_ref[c, rows, :] - shift).astype(_BF16)
        acc_ref[c] = alpha * acc_ref[c] + _dot(vT_ref[0, hd, j], p_ref[c])
        m_ref[c] = m_new

    for c, (qb, hd) in enumerate(chains):
        pair = hd // HEADS_PER_VREG
        qh = qT_ref[0, hd, qb]
        km = kmean_ref[0, :, pair * LANES:(pair + 1) * LANES].astype(_BF16)
        avail = blk_id < (i0 + qb).astype(_F32)
        gate = jnp.where(avail, _dot(km, qh), NEG_BIG)
        sel = jnp.zeros((nb, blk), _F32)
        for _ in range(MOBA_TOPK):
            best = jnp.max(gate, axis=0, keepdims=True)
            lowest = jnp.min(jnp.where(gate == best, blk_id, float(nb)), axis=0, keepdims=True)
            take = (blk_id == lowest) & avail
            sel = jnp.where(take, 1.0, sel)
            avail = avail & jnp.logical_not(take)
            gate = jnp.where(take, NEG_BIG, gate)
        for n in range(nb):
            sel_ref[c, n] = sel[n:n + 1, :]
        sel_ref[c, own] = jnp.ones((1, blk), _F32)
        m_ref[c] = jnp.full((1, blk), NEG_BIG, _F32)
        acc_ref[c] = jnp.zeros((V_ROWS, blk), _F32)
        stage(c, i0 + qb, masked=True)

    def sweep_step(u, carry):
        for c, (qb, hd) in enumerate(chains):
            process(c, jnp.where(u == 0, own, u - 1), jnp.where(u == 0, i0 + qb, u - 1))
            stage(c, u, masked=False)
        return carry

    lax.fori_loop(0, i0, sweep_step, 0)

    for c, (qb, hd) in enumerate(chains):
        process(c, jnp.where(i0 == 0, own, i0 - 1), jnp.where(i0 == 0, i0 + qb, i0 - 1))
        if qb == 1:
            stage(c, i0, masked=False)

    @pl.when(i0 >= 0)
    def _second_query_block_tail():
        for c, (qb, hd) in enumerate(chains):
            if qb == 1:
                process(c, i0, i0)

    for c, (qb, hd) in enumerate(chains):
        acc = acc_ref[c]
        denom = acc[HEAD_DIM:HEAD_DIM + 1, :]
        o_ref[0, hd, qb] = (acc[0:HEAD_DIM, :] * (1.0 / denom)).astype(_BF16)


def _attn_call(qT, k, vT, kmean):
    B, H, nb, _, blk = qT.shape
    S = k.shape[1]
    n_chains = Q_BLOCKS * H
    return pl.pallas_call(
        _attn_kernel,
        grid=(B, nb // Q_BLOCKS),
        in_specs=[
            pl.BlockSpec((1, H, Q_BLOCKS, LANES, blk), lambda b, g: (b, 0, g, 0, 0)),
            pl.BlockSpec((1, S, ATTN_WIDTH), lambda b, g: (b, 0, 0)),
            pl.BlockSpec((1, H, nb, V_ROWS, blk), lambda b, g: (b, 0, 0, 0, 0)),
            pl.BlockSpec((1, nb, ATTN_WIDTH), lambda b, g: (b, 0, 0)),
        ],
        out_specs=pl.BlockSpec((1, H, Q_BLOCKS, HEAD_DIM, blk), lambda b, g: (b, 0, g, 0, 0)),
        out_shape=jax.ShapeDtypeStruct((B, H, nb, HEAD_DIM, blk), _BF16),
        scratch_shapes=[
            pltpu.VMEM((n_chains, V_ROWS, blk), _F32),
            pltpu.VMEM((n_chains, 1, blk), _F32),
            pltpu.VMEM((n_chains, nb + 1, 1, blk), _F32),
            pltpu.VMEM((n_chains, blk, blk), _F32),
            pltpu.VMEM((n_chains, 1, blk), _F32),
            pltpu.VMEM((n_chains, blk, blk), _BF16),
        ],
        compiler_params=pltpu.CompilerParams(
            dimension_semantics=("arbitrary", "arbitrary"), vmem_limit_bytes=VMEM_LIMIT),
        name="moba_attn",
    )(qT, k, vT, kmean)


def _out_kernel(x_ref, oT_ref, hg0_ref, hgn_ref, halon_ref, dww_ref, dwb_ref, lng_ref, lnb_ref, wout_ref,
                g2_ref, wg_ref, wu_ref, wd_ref, y_ref, hbuf_ref, hshift_ref, wtap_ref, cact_ref,
                *, tiles_per_seq):
    tm = x_ref.shape[1]
    nib = tm // MOBA_BLOCK
    n = pl.program_id(0)

    n_chunks = tm // CONV_ROWS

    def exact_zero(v):
        return jnp.minimum(jnp.abs(v[0:SUBLANES, 0:LANES]), 0.0)

    def add_to_corner(v, z):
        top = jnp.concatenate([v[0:SUBLANES, 0:LANES] + z, v[0:SUBLANES, LANES:]], axis=1)
        return top if v.shape[0] == SUBLANES else jnp.concatenate([top, v[SUBLANES:]], axis=0)

    def conv_setup(hg_tile, halo):
        hbuf_ref[0:CONV_HALO, :] = halo
        hbuf_ref[CONV_HALO:CONV_HALO + tm, :] = hg_tile
        span = hshift_ref.shape[1]
        for r in range(1, SUBLANES):
            hshift_ref[r - 1] = hbuf_ref[r:r + span, :]
        for kk in range(CONV_KERNEL):
            wtap_ref[kk] = jnp.broadcast_to(dww_ref[kk:kk + 1, :], (SUBLANES, CONV_WIDTH))

    def conv_chunk(ci, after=None):
        c0 = ci * CONV_ROWS
        first = CONV_HALO - (CONV_KERNEL - 1)
        accs = [None] * (CONV_ROWS // SUBLANES)
        for kk in range(CONV_KERNEL):
            r = (first + kk) % SUBLANES
            lo = c0 + first + kk - r
            src = hbuf_ref if r == 0 else hshift_ref.at[r - 1]
            wk = wtap_ref[kk]
            for gi in range(len(accs)):
                term = wk * src[lo + gi * SUBLANES:lo + (gi + 1) * SUBLANES, :]
                if accs[gi] is None:
                    accs[gi] = term if after is None else add_to_corner(term, after)
                else:
                    accs[gi] = accs[gi] + term
        conv = jnp.concatenate(accs, axis=0) + dwb_ref[...]
        mu = jnp.mean(conv, axis=-1, keepdims=True)
        cen = conv - mu
        var = jnp.mean(cen * cen, axis=-1, keepdims=True)
        cn = cen * lax.rsqrt(var + EPS) * lng_ref[...] + lnb_ref[...]
        cact_ref[c0:c0 + CONV_ROWS, :] = (cn * jax.nn.sigmoid(cn)).astype(_BF16)
        return exact_zero(cn)

    @pl.when(n == 0)
    def _first_tile():
        conv_setup(hg0_ref[0], jnp.zeros((CONV_HALO, CONV_WIDTH), _F32))
        for ci in range(n_chunks):
            conv_chunk(ci)

    cact = cact_ref[...]
    nxt = jnp.minimum(n + 1, pl.num_programs(0) - 1)
    halo = halon_ref[0]
    conv_setup(hgn_ref[0], jnp.where(nxt % tiles_per_seq == 0, jnp.zeros_like(halo), halo))

    finished = []

    def dot_after_chunk(a, b):
        d = len(finished)
        out = _dot(a, b)
        if d < n_chunks:
            out = add_to_corner(out, conv_chunk(d, finished[d - CONV_CHAINS] if d >= CONV_CHAINS else None))
        finished.append(exact_zero(out))
        return out

    mix = _dot(cact, wout_ref[ATTN_WIDTH:ATTN_WIDTH + CONV_WIDTH, :])
    attn_parts = []
    for ib in range(nib):
        oT = oT_ref[0, :, ib].reshape(ATTN_WIDTH, MOBA_BLOCK)
        attn_parts.append(lax.dot_general(oT, wout_ref[0:ATTN_WIDTH, :], (((0,), (0,)), ((), ())),
                                          preferred_element_type=_F32))
    x1 = x_ref[0] + (mix + jnp.concatenate(attn_parts, axis=0))

    ms = jnp.mean(x1 * x1, axis=-1, keepdims=True)
    h2 = (x1 * lax.rsqrt(ms + EPS) * g2_ref[...]).astype(_BF16)
    y = x1
    for c0 in range(0, D_FF, FF_CHUNK):
        cs = slice(c0, min(c0 + FF_CHUNK, D_FF))
        gt = dot_after_chunk(h2, wg_ref[:, cs])
        up = dot_after_chunk(h2, wu_ref[:, cs])
        act = (gt * jax.nn.sigmoid(gt) * up).astype(_BF16)
        y = y + dot_after_chunk(act, wd_ref[cs, :])
    assert len(finished) >= n_chunks, "every conv chunk must be tied to a matmul"
    y_ref[0] = y


def _out_call(x, oT, hg, dww, dwb, lng, lnb, wout, g2, wg, wu, wd):
    B, S, D = x.shape
    tm = TM_OUT
    nib = tm // MOBA_BLOCK
    H = N_HEADS
    const = lambda n: (0, 0)
    resident = functools.partial(pl.BlockSpec, pipeline_mode=pl.Buffered(1))
    halo_blocks = tm // CONV_HALO
    nt = S // tm
    n_steps = B * nt

    def next_tile(n):
        nxt = jnp.minimum(n + 1, n_steps - 1)
        return nxt // nt, nxt % nt

    def next_hg(n):
        b, t = next_tile(n)
        return b, t, 0

    def next_halo(n):
        b, t = next_tile(n)
        return b, jnp.maximum(t * halo_blocks - 1, 0), 0

    return pl.pallas_call(
        functools.partial(_out_kernel, tiles_per_seq=nt),
        grid=(n_steps,),
        in_specs=[
            pl.BlockSpec((1, tm, D), lambda n: (n // nt, n % nt, 0)),
            pl.BlockSpec((1, H, nib, HEAD_DIM, MOBA_BLOCK), lambda n: (n // nt, 0, n % nt, 0, 0)),
            pl.BlockSpec((1, tm, CONV_WIDTH), lambda n: (0, 0, 0)),
            pl.BlockSpec((1, tm, CONV_WIDTH), next_hg),
            pl.BlockSpec((1, CONV_HALO, CONV_WIDTH), next_halo),
            pl.BlockSpec((CONV_KERNEL, CONV_WIDTH), const),
            pl.BlockSpec((1, CONV_WIDTH), const),
            pl.BlockSpec((1, CONV_WIDTH), const),
            pl.BlockSpec((1, CONV_WIDTH), const),
            resident((D, D), const),
            pl.BlockSpec((1, D), const),
            resident((D, D_FF), const),
            resident((D, D_FF), const),
            resident((D_FF, D), const),
        ],
        out_specs=pl.BlockSpec((1, tm, D), lambda n: (n // nt, n % nt, 0)),
        out_shape=jax.ShapeDtypeStruct((B, S, D), _F32),
        scratch_shapes=[
            pltpu.VMEM((CONV_HALO + tm, CONV_WIDTH), _F32),
            pltpu.VMEM((SUBLANES - 1, CONV_HALO + tm - SUBLANES, CONV_WIDTH), _F32),
            pltpu.VMEM((CONV_KERNEL, SUBLANES, CONV_WIDTH), _F32),
            pltpu.VMEM((tm, CONV_WIDTH), _BF16),
        ],
        compiler_params=pltpu.CompilerParams(
            dimension_semantics=("arbitrary",), vmem_limit_bytes=VMEM_LIMIT),
        name="moba_out_ffn",
    )(x, oT, hg, hg, hg, dww, dwb, lng, lnb, wout, g2, wg, wu, wd)


def _rope_tables(seq_len):
    pos = jnp.arange(seq_len, dtype=_F32)
    inv_freq = ROPE_THETA ** (-jnp.arange(0, HEAD_DIM, 2, dtype=_F32) / HEAD_DIM)
    ang = pos[:, None] * inv_freq[None, :]
    ang = jnp.concatenate([ang, ang], axis=-1)
    sign = jnp.where(jnp.arange(HEAD_DIM) < HEAD_DIM // 2, -1.0, 1.0).astype(_F32)
    cos2 = jnp.tile(jnp.cos(ang), (1, HEADS_PER_VREG))
    sin2 = jnp.tile(jnp.sin(ang) * sign[None, :], (1, HEADS_PER_VREG))
    return cos2, sin2


def _layer(x, norm1_g, w_in, glu_b, q_norm_g, k_norm_g, dw_w, dw_b, conv_ln_g, conv_ln_b,
           w_out, norm2_g, w_gate, w_up, w_down, cos2, sin2, gmat):
    B, S, _ = x.shape
    row = lambda a: a.reshape(1, -1)
    qT, k, vT, kmean, hg, w_out_bf, w_gate_bf, w_up_bf, w_down_bf = _proj_call(
        x, row(norm1_g), w_in, row(glu_b),
        row(jnp.tile(q_norm_g, N_HEADS)), row(jnp.tile(k_norm_g, N_HEADS)), gmat, cos2, sin2,
        [w_out, w_gate, w_up, w_down.reshape(D_MODEL, D_FF)])
    kmean = kmean.reshape(B, S // MOBA_BLOCK, ATTN_WIDTH)
    oT = _attn_call(qT, k, vT, kmean)
    return _out_call(x, oT, hg, dw_w, row(dw_b), row(conv_ln_g), row(conv_ln_b),
                     w_out_bf, row(norm2_g), w_gate_bf, w_up_bf, w_down_bf.reshape(D_FF, D_MODEL))


def kernel(x, norm1_g, w_in, glu_b, q_norm_g, k_norm_g, dw_w, dw_b, conv_ln_g, conv_ln_b, w_out,
           norm2_g, w_gate, w_up, w_down):
    S = x.shape[1]
    cos2, sin2 = _rope_tables(S)
    head_of = jnp.arange(ATTN_WIDTH) // HEAD_DIM
    gmat = jnp.where(head_of[:, None] == head_of[None, :], 1.0 / HEAD_DIM, 0.0).astype(_BF16)
    for l in range(norm1_g.shape[0]):
        x = _layer(x, norm1_g[l], w_in[l], glu_b[l], q_norm_g[l], k_norm_g[l], dw_w[l], dw_b[l],
                   conv_ln_g[l], conv_ln_b[l], w_out[l], norm2_g[l], w_gate[l], w_up[l], w_down[l],
                   cos2, sin2, gmat)
    return x
```

```python
import functools

import jax
import jax.numpy as jnp
from jax import lax
from jax.experimental import pallas as pl
from jax.experimental.pallas import tpu as pltpu

D_MODEL = 1024
ATTN_WIDTH = 512
CONV_WIDTH = 512
N_HEADS = 8
HEAD_DIM = 64
CONV_KERNEL = 31
MOBA_BLOCK = 256
MOBA_TOPK = 3
ROPE_THETA = 10000.0
D_FF = 2816
EPS = 1e-6
D_IN = 3 * ATTN_WIDTH + 2 * CONV_WIDTH

LANES = 128
HEADS_PER_VREG = LANES // HEAD_DIM
V_ROWS = 80
Q_BLOCKS = 2
EXP_ROWS = 16
CONV_HALO = 32
LOG2_E = 1.4426950408889634
Q_SCALE = HEAD_DIM ** -0.5 * LOG2_E
NEG_BIG = -1e30
POS_BIG = 1e30

TM_PROJ = 512
TM_OUT = 512
FF_CHUNK = 256
SUBLANES = 8
CONV_ROWS = 16
CONV_CHAINS = 4
VMEM_LIMIT = 56 * 1024 * 1024

_BF16 = jnp.bfloat16
_F32 = jnp.float32


def _dot(a, b):
    return jnp.dot(a, b, preferred_element_type=_F32)


def _proj_kernel(x_ref, g1_ref, win_ref, glub_ref, gq_ref, gk_ref, gmat_ref, cos_ref, sin_ref,
                 wout_f32, wgate_f32, wup_f32, wdown_f32,
                 qT_ref, k_ref, vT_ref, kmean_ref, hg_ref, wout_bf, wgate_bf, wup_bf, wdown_bf,
                 winbf_ref):
    tm = x_ref.shape[1]
    nib = tm // MOBA_BLOCK

    step = pl.program_id(0) * pl.num_programs(1) + pl.program_id(1)

    @pl.when(step == 0)
    def _cast_w_in():
        for c0 in range(0, D_IN, ATTN_WIDTH):
            winbf_ref[:, c0:c0 + ATTN_WIDTH] = win_ref[0, :, c0:c0 + ATTN_WIDTH].astype(_BF16)

    @pl.when(step % 2 == 0)
    def _cast_w_down():
        wdown_bf[...] = wdown_f32[0].astype(_BF16)

    for src, dst in ((wout_f32, wout_bf), (wgate_f32, wgate_bf), (wup_f32, wup_bf)):
        dst[...] = src[0].astype(_BF16)
    win_ref = winbf_ref

    x = x_ref[0]
    ms = jnp.mean(x * x, axis=-1, keepdims=True)
    h = (x * lax.rsqrt(ms + EPS) * g1_ref[...]).astype(_BF16)

    cos = cos_ref[...]
    sin = sin_ref[...]
    lane = lax.broadcasted_iota(jnp.int32, (tm, LANES), 1)
    first_half = (lane & (HEAD_DIM // 2)) == 0

    def head_norm_rope(p, g_ref):
        msq = _dot((p * p).astype(_BF16), gmat_ref[...])
        pn = p * lax.rsqrt(msq + EPS) * g_ref[...]
        outs = []
        for c in range(ATTN_WIDTH // LANES):
            xc = pn[:, c * LANES:(c + 1) * LANES]
            partner = jnp.where(first_half,
                                pltpu.roll(xc, LANES - HEAD_DIM // 2, 1),
                                pltpu.roll(xc, HEAD_DIM // 2, 1))
            outs.append(xc * cos + partner * sin)
        return jnp.concatenate(outs, axis=1)

    aw = ATTN_WIDTH
    q = head_norm_rope(_dot(h, win_ref[:, 0:aw]), gq_ref) * Q_SCALE
    k = head_norm_rope(_dot(h, win_ref[:, aw:2 * aw]), gk_ref)
    v = _dot(h, win_ref[:, 2 * aw:3 * aw])

    k_ref[0] = k.astype(_BF16)
    for ib in range(nib):
        kmean_ref[0, 0, ib:ib + 1, :] = jnp.mean(
            k[ib * MOBA_BLOCK:(ib + 1) * MOBA_BLOCK], axis=0, keepdims=True)

    qT = q.T.astype(_BF16)
    vT = v.T.astype(_BF16)
    zeros = jnp.zeros((HEAD_DIM, MOBA_BLOCK), _BF16)
    pad_rows = lax.broadcasted_iota(jnp.int32, (V_ROWS - HEAD_DIM, MOBA_BLOCK), 0)
    ones_row = jnp.where(pad_rows == 0, 1.0, 0.0).astype(_BF16)
    for hd in range(N_HEADS):
        lo = (hd % HEADS_PER_VREG) * HEAD_DIM
        for ib in range(nib):
            cols = slice(ib * MOBA_BLOCK, (ib + 1) * MOBA_BLOCK)
            rows = slice(hd * HEAD_DIM, (hd + 1) * HEAD_DIM)
            qT_ref[0, hd, ib, lo:lo + HEAD_DIM, :] = qT[rows, cols]
            qT_ref[0, hd, ib, HEAD_DIM - lo:2 * HEAD_DIM - lo, :] = zeros
            vT_ref[0, hd, ib, 0:HEAD_DIM, :] = vT[rows, cols]
            vT_ref[0, hd, ib, HEAD_DIM:V_ROWS, :] = ones_row

    cw = CONV_WIDTH
    a = _dot(h, win_ref[:, 3 * aw:3 * aw + cw]) + glub_ref[:, 0:cw]
    g = _dot(h, win_ref[:, 3 * aw + cw:3 * aw + 2 * cw]) + glub_ref[:, cw:2 * cw]
    hg_ref[0] = a * jax.nn.sigmoid(g)


def _proj_call(x, g1, win, glub, gq, gk, gmat, cos2, sin2, layer, wout, wgate, wup, wdown):
    B, S, D = x.shape
    tm = TM_PROJ
    nib = tm // MOBA_BLOCK
    nb = S // MOBA_BLOCK
    nt = S // tm
    n_steps = B * nt
    bf16_rows = 16
    const = lambda b, t: (0, 0)

    def slab_specs(w, steps_per_slab):
        rows = w.shape[1] * steps_per_slab // n_steps
        assert rows * n_steps == w.shape[1] * steps_per_slab and rows % bf16_rows == 0
        return (pl.BlockSpec((1, rows, w.shape[2]), lambda b, t: (layer, (b * nt + t) // steps_per_slab, 0)),
                pl.BlockSpec((rows, w.shape[2]), lambda b, t: ((b * nt + t) // steps_per_slab, 0)))

    later = [slab_specs(wout, 1), slab_specs(wgate, 1), slab_specs(wup, 1), slab_specs(wdown, 2)]
    return pl.pallas_call(
        _proj_kernel,
        grid=(B, nt),
        in_specs=[
            pl.BlockSpec((1, tm, D), lambda b, t: (b, t, 0)),
            pl.BlockSpec((1, D), const),
            pl.BlockSpec((1, D, D_IN), lambda b, t: (layer, 0, 0), pipeline_mode=pl.Buffered(1)),
            pl.BlockSpec((1, 2 * CONV_WIDTH), const),
            pl.BlockSpec((1, ATTN_WIDTH), const),
            pl.BlockSpec((1, ATTN_WIDTH), const),
            pl.BlockSpec((ATTN_WIDTH, ATTN_WIDTH), const),
            pl.BlockSpec((tm, LANES), lambda b, t: (t, 0)),
            pl.BlockSpec((tm, LANES), lambda b, t: (t, 0)),
        ] + [spec_in for spec_in, _ in later],
        out_specs=[
            pl.BlockSpec((1, N_HEADS, nib, LANES, MOBA_BLOCK), lambda b, t: (b, 0, t, 0, 0)),
            pl.BlockSpec((1, tm, ATTN_WIDTH), lambda b, t: (b, t, 0)),
            pl.BlockSpec((1, N_HEADS, nib, V_ROWS, MOBA_BLOCK), lambda b, t: (b, 0, t, 0, 0)),
            pl.BlockSpec((1, 1, nib, ATTN_WIDTH), lambda b, t: (b, t, 0, 0)),
            pl.BlockSpec((1, tm, CONV_WIDTH), lambda b, t: (b, t, 0)),
        ] + [spec_out for _, spec_out in later],
        out_shape=[
            jax.ShapeDtypeStruct((B, N_HEADS, nb, LANES, MOBA_BLOCK), _BF16),
            jax.ShapeDtypeStruct((B, S, ATTN_WIDTH), _BF16),
            jax.ShapeDtypeStruct((B, N_HEADS, nb, V_ROWS, MOBA_BLOCK), _BF16),
            jax.ShapeDtypeStruct((B, S // tm, nib, ATTN_WIDTH), _F32),
            jax.ShapeDtypeStruct((B, S, CONV_WIDTH), _F32),
        ] + [jax.ShapeDtypeStruct(w.shape[1:], _BF16) for w in (wout, wgate, wup, wdown)],
        scratch_shapes=[pltpu.VMEM((D, D_IN), _BF16)],
        compiler_params=pltpu.CompilerParams(
            dimension_semantics=("arbitrary", "arbitrary"), vmem_limit_bytes=VMEM_LIMIT),
        name="moba_proj",
    )(x, g1, win, glub, gq, gk, gmat, cos2, sin2, wout, wgate, wup, wdown)


def _attn_kernel(qT_ref, k_ref, vT_ref, kmean_ref, o_ref,
                 acc_ref, m_ref, sel_ref, s_ref, smax_ref, p_ref):
    assert Q_BLOCKS == 2
    i0 = pl.program_id(1) * Q_BLOCKS
    nb = kmean_ref.shape[1]
    blk = MOBA_BLOCK
    own = nb
    row = lax.broadcasted_iota(jnp.int32, (blk, blk), 0)
    col = lax.broadcasted_iota(jnp.int32, (blk, blk), 1)
    causal = row <= col
    blk_id = lax.broadcasted_iota(jnp.int32, (nb, blk), 0).astype(_F32)
    chains = [(qb, hd) for qb in range(Q_BLOCKS) for hd in range(N_HEADS)]

    def k_block(j, pair):
        start = pl.multiple_of(j * blk, blk)
        return k_ref[0, pl.ds(start, blk), pair * LANES:(pair + 1) * LANES]

    def stage(c, j, masked):
        qb, hd = chains[c]
        s = _dot(k_block(j, hd // HEADS_PER_VREG), qT_ref[0, hd, qb])
        if masked:
            s = jnp.where(causal, s, NEG_BIG)
        s_ref[c] = s
        smax_ref[c] = jnp.max(s, axis=0, keepdims=True)

    def process(c, sel_row, j):
        qb, hd = chains[c]
        on = sel_ref[c, sel_row] > 0.5
        m_old = m_ref[c]
        m_new = jnp.maximum(m_old, jnp.where(on, smax_ref[c], NEG_BIG))
        alpha = jnp.exp2(m_old - m_new)
        shift = jnp.broadcast_to(jnp.where(on, m_new, POS_BIG), (EXP_ROWS, blk))
        for r in range(blk // EXP_ROWS):
            rows = slice(r * EXP_ROWS, (r + 1) * EXP_ROWS)
            p_ref[c, rows, :] = jnp.exp2(s_ref[c, rows, :] - shift).astype(_BF16)
        acc_ref[c] = alpha * acc_ref[c] + _dot(vT_ref[0, hd, j], p_ref[c])
        m_ref[c] = m_new

    for c, (qb, hd) in enumerate(chains):
        pair = hd // HEADS_PER_VREG
        qh = qT_ref[0, hd, qb]
        km = kmean_ref[0, :, pair * LANES:(pair + 1) * LANES].astype(_BF16)
        avail = blk_id < (i0 + qb).astype(_F32)
        gate = jnp.where(avail, _dot(km, qh), NEG_BIG)
        sel = jnp.zeros((nb, blk), _F32)
        for _ in range(MOBA_TOPK):
            best = jnp.max(gate, axis=0, keepdims=True)
            lowest = jnp.min(jnp.where(gate == best, blk_id, float(nb)), axis=0, keepdims=True)
            take = (blk_id == lowest) & avail
            sel = jnp.where(take, 1.0, sel)
            avail = avail & jnp.logical_not(take)
            gate = jnp.where(take, NEG_BIG, gate)
        for n in range(nb):
            sel_ref[c, n] = sel[n:n + 1, :]
        sel_ref[c, own] = jnp.ones((1, blk), _F32)
        m_ref[c] = jnp.full((1, blk), NEG_BIG, _F32)
        acc_ref[c] = jnp.zeros((V_ROWS, blk), _F32)
        stage(c, i0 + qb, masked=True)

    def sweep_step(u, carry):
        for c, (qb, hd) in enumerate(chains):
            process(c, jnp.where(u == 0, own, u - 1), jnp.where(u == 0, i0 + qb, u - 1))
            stage(c, u, masked=False)
        return carry

    lax.fori_loop(0, i0, sweep_step, 0)

    for c, (qb, hd) in enumerate(chains):
        process(c, jnp.where(i0 == 0, own, i0 - 1), jnp.where(i0 == 0, i0 + qb, i0 - 1))
        if qb == 1:
            stage(c, i0, masked=False)

    @pl.when(i0 >= 0)
    def _second_query_block_tail():
        for c, (qb, hd) in enumerate(chains):
            if qb == 1:
                process(c, i0, i0)

    for c, (qb, hd) in enumerate(chains):
        acc = acc_ref[c]
        denom = acc[HEAD_DIM:HEAD_DIM + 1, :]
        o_ref[0, hd, qb] = (acc[0:HEAD_DIM, :] * (1.0 / denom)).astype(_BF16)


def _attn_call(qT, k, vT, kmean):
    B, H, nb, _, blk = qT.shape
    S = k.shape[1]
    n_chains = Q_BLOCKS * H
    return pl.pallas_call(
        _attn_kernel,
        grid=(B, nb // Q_BLOCKS),
        in_specs=[
            pl.BlockSpec((1, H, Q_BLOCKS, LANES, blk), lambda b, g: (b, 0, g, 0, 0)),
            pl.BlockSpec((1, S, ATTN_WIDTH), lambda b, g: (b, 0, 0)),
            pl.BlockSpec((1, H, nb, V_ROWS, blk), lambda b, g: (b, 0, 0, 0, 0)),
            pl.BlockSpec((1, nb, ATTN_WIDTH), lambda b, g: (b, 0, 0)),
        ],
        out_specs=pl.BlockSpec((1, H, Q_BLOCKS, HEAD_DIM, blk), lambda b, g: (b, 0, g, 0, 0)),
        out_shape=jax.ShapeDtypeStruct((B, H, nb, HEAD_DIM, blk), _BF16),
        scratch_shapes=[
            pltpu.VMEM((n_chains, V_ROWS, blk), _F32),
            pltpu.VMEM((n_chains, 1, blk), _F32),
            pltpu.VMEM((n_chains, nb + 1, 1, blk), _F32),
            pltpu.VMEM((n_chains, blk, blk), _F32),
            pltpu.VMEM((n_chains, 1, blk), _F32),
            pltpu.VMEM((n_chains, blk, blk), _BF16),
        ],
        compiler_params=pltpu.CompilerParams(
            dimension_semantics=("arbitrary", "arbitrary"), vmem_limit_bytes=VMEM_LIMIT),
        name="moba_attn",
    )(qT, k, vT, kmean)


def _out_kernel(x_ref, oT_ref, hg0_ref, hgn_ref, halon_ref, dww_ref, dwb_ref, lng_ref, lnb_ref, wout_ref,
                g2_ref, wg_ref, wu_ref, wd_ref, y_ref, hbuf_ref, hshift_ref, wtap_ref, cact_ref,
                *, tiles_per_seq):
    tm = x_ref.shape[1]
    nib = tm // MOBA_BLOCK
    n = pl.program_id(0)

    n_chunks = tm // CONV_ROWS

    def exact_zero(v):
        return jnp.minimum(jnp.abs(v[0:SUBLANES, 0:LANES]), 0.0)

    def add_to_corner(v, z):
        top = jnp.concatenate([v[0:SUBLANES, 0:LANES] + z, v[0:SUBLANES, LANES:]], axis=1)
        return top if v.shape[0] == SUBLANES else jnp.concatenate([top, v[SUBLANES:]], axis=0)

    def conv_setup(hg_tile, halo):
        hbuf_ref[0:CONV_HALO, :] = halo
        hbuf_ref[CONV_HALO:CONV_HALO + tm, :] = hg_tile
        span = hshift_ref.shape[1]
        for r in range(1, SUBLANES):
            hshift_ref[r - 1] = hbuf_ref[r:r + span, :]
        for kk in range(CONV_KERNEL):
            wtap_ref[kk] = jnp.broadcast_to(dww_ref[kk:kk + 1, :], (SUBLANES, CONV_WIDTH))

    def conv_chunk(ci, after=None):
        c0 = ci * CONV_ROWS
        first = CONV_HALO - (CONV_KERNEL - 1)
        accs = [None] * (CONV_ROWS // SUBLANES)
        for kk in range(CONV_KERNEL):
            r = (first + kk) % SUBLANES
            lo = c0 + first + kk - r
            src = hbuf_ref if r == 0 else hshift_ref.at[r - 1]
            wk = wtap_ref[kk]
            for gi in range(len(accs)):
                term = wk * src[lo + gi * SUBLANES:lo + (gi + 1) * SUBLANES, :]
                if accs[gi] is None:
                    accs[gi] = term if after is None else add_to_corner(term, after)
                else:
                    accs[gi] = accs[gi] + term
        conv = jnp.concatenate(accs, axis=0) + dwb_ref[...]
        mu = jnp.mean(conv, axis=-1, keepdims=True)
        cen = conv - mu
        var = jnp.mean(cen * cen, axis=-1, keepdims=True)
        cn = cen * lax.rsqrt(var + EPS) * lng_ref[...] + lnb_ref[...]
        cact_ref[c0:c0 + CONV_ROWS, :] = (cn * jax.nn.sigmoid(cn)).astype(_BF16)
        return exact_zero(cn)

    @pl.when(n == 0)
    def _first_tile():
        conv_setup(hg0_ref[0], jnp.zeros((CONV_HALO, CONV_WIDTH), _F32))
        for ci in range(n_chunks):
            conv_chunk(ci)

    cact = cact_ref[...]
    nxt = jnp.minimum(n + 1, pl.num_programs(0) - 1)
    halo = halon_ref[0]
    conv_setup(hgn_ref[0], jnp.where(nxt % tiles_per_seq == 0, jnp.zeros_like(halo), halo))

    finished = []

    def dot_after_chunk(a, b):
        d = len(finished)
        out = _dot(a, b)
        if d < n_chunks:
            out = add_to_corner(out, conv_chunk(d, finished[d - CONV_CHAINS] if d >= CONV_CHAINS else None))
        finished.append(exact_zero(out))
        return out

    mix = _dot(cact, wout_ref[ATTN_WIDTH:ATTN_WIDTH + CONV_WIDTH, :])
    attn_parts = []
    for ib in range(nib):
        oT = oT_ref[0, :, ib].reshape(ATTN_WIDTH, MOBA_BLOCK)
        attn_parts.append(lax.dot_general(oT, wout_ref[0:ATTN_WIDTH, :], (((0,), (0,)), ((), ())),
                                          preferred_element_type=_F32))
    x1 = x_ref[0] + (mix + jnp.concatenate(attn_parts, axis=0))

    ms = jnp.mean(x1 * x1, axis=-1, keepdims=True)
    h2 = (x1 * lax.rsqrt(ms + EPS) * g2_ref[...]).astype(_BF16)
    y = x1
    for c0 in range(0, D_FF, FF_CHUNK):
        cs = slice(c0, min(c0 + FF_CHUNK, D_FF))
        gt = dot_after_chunk(h2, wg_ref[:, cs])
        up = dot_after_chunk(h2, wu_ref[:, cs])
        act = (gt * jax.nn.sigmoid(gt) * up).astype(_BF16)
        y = y + dot_after_chunk(act, wd_ref[cs, :])
    assert len(finished) >= n_chunks, "every conv chunk must be tied to a matmul"
    y_ref[0] = y


def _out_call(x, oT, hg, dww, dwb, lng, lnb, wout, g2, wg, wu, wd):
    B, S, D = x.shape
    tm = TM_OUT
    nib = tm // MOBA_BLOCK
    H = N_HEADS
    const = lambda n: (0, 0)
    resident = functools.partial(pl.BlockSpec, pipeline_mode=pl.Buffered(1))
    halo_blocks = tm // CONV_HALO
    nt = S // tm
    n_steps = B * nt

    def next_tile(n):
        nxt = jnp.minimum(n + 1, n_steps - 1)
        return nxt // nt, nxt % nt

    def next_hg(n):
        b, t = next_tile(n)
        return b, t, 0

    def next_halo(n):
        b, t = next_tile(n)
        return b, jnp.maximum(t * halo_blocks - 1, 0), 0

    return pl.pallas_call(
        functools.partial(_out_kernel, tiles_per_seq=nt),
        grid=(n_steps,),
        in_specs=[
            pl.BlockSpec((1, tm, D), lambda n: (n // nt, n % nt, 0)),
            pl.BlockSpec((1, H, nib, HEAD_DIM, MOBA_BLOCK), lambda n: (n // nt, 0, n % nt, 0, 0)),
            pl.BlockSpec((1, tm, CONV_WIDTH), lambda n: (0, 0, 0)),
            pl.BlockSpec((1, tm, CONV_WIDTH), next_hg),
            pl.BlockSpec((1, CONV_HALO, CONV_WIDTH), next_halo),
            pl.BlockSpec((CONV_KERNEL, CONV_WIDTH), const),
            pl.BlockSpec((1, CONV_WIDTH), const),
            pl.BlockSpec((1, CONV_WIDTH), const),
            pl.BlockSpec((1, CONV_WIDTH), const),
            resident((D, D), const),
            pl.BlockSpec((1, D), const),
            resident((D, D_FF), const),
            resident((D, D_FF), const),
            resident((D_FF, D), const),
        ],
        out_specs=pl.BlockSpec((1, tm, D), lambda n: (n // nt, n % nt, 0)),
        out_shape=jax.ShapeDtypeStruct((B, S, D), _F32),
        scratch_shapes=[
            pltpu.VMEM((CONV_HALO + tm, CONV_WIDTH), _F32),
            pltpu.VMEM((SUBLANES - 1, CONV_HALO + tm - SUBLANES, CONV_WIDTH), _F32),
            pltpu.VMEM((CONV_KERNEL, SUBLANES, CONV_WIDTH), _F32),
            pltpu.VMEM((tm, CONV_WIDTH), _BF16),
        ],
        compiler_params=pltpu.CompilerParams(
            dimension_semantics=("arbitrary",), vmem_limit_bytes=VMEM_LIMIT),
        name="moba_out_ffn",
    )(x, oT, hg, hg, hg, dww, dwb, lng, lnb, wout, g2, wg, wu, wd)


def _rope_tables(seq_len):
    pos = jnp.arange(seq_len, dtype=_F32)
    inv_freq = ROPE_THETA ** (-jnp.arange(0, HEAD_DIM, 2, dtype=_F32) / HEAD_DIM)
    ang = pos[:, None] * inv_freq[None, :]
    ang = jnp.concatenate([ang, ang], axis=-1)
    sign = jnp.where(jnp.arange(HEAD_DIM) < HEAD_DIM // 2, -1.0, 1.0).astype(_F32)
    cos2 = jnp.tile(jnp.cos(ang), (1, HEADS_PER_VREG))
    sin2 = jnp.tile(jnp.sin(ang) * sign[None, :], (1, HEADS_PER_VREG))
    return cos2, sin2


def _layer(x, layer, norm1_g, w_in, glu_b, q_norm_g, k_norm_g, dw_w, dw_b, conv_ln_g, conv_ln_b,
           w_out, norm2_g, w_gate, w_up, w_down, cos2, sin2, gmat):
    B, S, _ = x.shape
    row = lambda a: a.reshape(1, -1)
    qT, k, vT, kmean, hg, w_out_bf, w_gate_bf, w_up_bf, w_down_bf = _proj_call(
        x, row(norm1_g), w_in, row(glu_b),
        row(jnp.tile(q_norm_g, N_HEADS)), row(jnp.tile(k_norm_g, N_HEADS)), gmat, cos2, sin2,
        layer, w_out, w_gate, w_up, w_down)
    kmean = kmean.reshape(B, S // MOBA_BLOCK, ATTN_WIDTH)
    oT = _attn_call(qT, k, vT, kmean)
    return _out_call(x, oT, hg, dw_w, row(dw_b), row(conv_ln_g), row(conv_ln_b),
                     w_out_bf, row(norm2_g), w_gate_bf, w_up_bf, w_down_bf)


def kernel(x, norm1_g, w_in, glu_b, q_norm_g, k_norm_g, dw_w, dw_b, conv_ln_g, conv_ln_b, w_out,
           norm2_g, w_gate, w_up, w_down):
    S = x.shape[1]
    cos2, sin2 = _rope_tables(S)
    head_of = jnp.arange(ATTN_WIDTH) // HEAD_DIM
    gmat = jnp.where(head_of[:, None] == head_of[None, :], 1.0 / HEAD_DIM, 0.0).astype(_BF16)
    for l in range(norm1_g.shape[0]):
        x = _layer(x, l, norm1_g[l], w_in, glu_b[l], q_norm_g[l], k_norm_g[l], dw_w[l], dw_b[l],
                   conv_ln_g[l], conv_ln_b[l], w_out, norm2_g[l], w_gate, w_up, w_down,
                   cos2, sin2, gmat)
    return x
```

```python
import functools

import jax
import jax.numpy as jnp
from jax import lax
from jax.experimental import pallas as pl
from jax.experimental.pallas import tpu as pltpu

D_MODEL = 1024
ATTN_WIDTH = 512
CONV_WIDTH = 512
N_HEADS = 8
HEAD_DIM = 64
CONV_KERNEL = 31
MOBA_BLOCK = 256
MOBA_TOPK = 3
ROPE_THETA = 10000.0
D_FF = 2816
EPS = 1e-6
D_IN = 3 * ATTN_WIDTH + 2 * CONV_WIDTH

LANES = 128
HEADS_PER_VREG = LANES // HEAD_DIM
V_ROWS = 80
Q_BLOCKS = 2
EXP_ROWS = 16
CONV_HALO = 32
LOG2_E = 1.4426950408889634
Q_SCALE = HEAD_DIM ** -0.5 * LOG2_E
NEG_BIG = -1e30
POS_BIG = 1e30

TM_PROJ = 512
TM_OUT = 512
FF_CHUNK = 256
SUBLANES = 8
CONV_ROWS = 16
CONV_CHAINS = 4
VMEM_LIMIT = 56 * 1024 * 1024

_BF16 = jnp.bfloat16
_F32 = jnp.float32


def _dot(a, b):
    return jnp.dot(a, b, preferred_element_type=_F32)


def _silu(x):
    half = 0.5 * x
    return half + half * jnp.tanh(half)


def _proj_kernel(x_ref, g1_ref, win_ref, glub_ref, gq_ref, gk_ref, gmat_ref, cos_ref, sin_ref,
                 wout_f32, wgate_f32, wup_f32, wdown_f32,
                 qT_ref, k_ref, vT_ref, kmean_ref, hg_ref, wout_bf, wgate_bf, wup_bf, wdown_bf,
                 winbf_ref):
    tm = x_ref.shape[1]
    nib = tm // MOBA_BLOCK

    step = pl.program_id(0) * pl.num_programs(1) + pl.program_id(1)

    @pl.when(step == 0)
    def _cast_w_in():
        for c0 in range(0, D_IN, ATTN_WIDTH):
            winbf_ref[:, c0:c0 + ATTN_WIDTH] = win_ref[0, :, c0:c0 + ATTN_WIDTH].astype(_BF16)

    @pl.when(step % 2 == 0)
    def _cast_w_down():
        wdown_bf[...] = wdown_f32[0].astype(_BF16)

    for src, dst in ((wout_f32, wout_bf), (wgate_f32, wgate_bf), (wup_f32, wup_bf)):
        dst[...] = src[0].astype(_BF16)
    win_ref = winbf_ref

    x = x_ref[0]
    ms = jnp.mean(x * x, axis=-1, keepdims=True)
    h = (x * lax.rsqrt(ms + EPS) * g1_ref[...]).astype(_BF16)

    cos = cos_ref[...]
    sin = sin_ref[...]
    lane = lax.broadcasted_iota(jnp.int32, (tm, LANES), 1)
    first_half = (lane & (HEAD_DIM // 2)) == 0

    def head_norm_rope(p, g_ref):
        msq = _dot((p * p).astype(_BF16), gmat_ref[...])
        pn = p * lax.rsqrt(msq + EPS) * g_ref[...]
        outs = []
        for c in range(ATTN_WIDTH // LANES):
            xc = pn[:, c * LANES:(c + 1) * LANES]
            partner = jnp.where(first_half,
                                pltpu.roll(xc, LANES - HEAD_DIM // 2, 1),
                                pltpu.roll(xc, HEAD_DIM // 2, 1))
            outs.append(xc * cos + partner * sin)
        return jnp.concatenate(outs, axis=1)

    aw = ATTN_WIDTH
    q = head_norm_rope(_dot(h, win_ref[:, 0:aw]), gq_ref) * Q_SCALE
    k = head_norm_rope(_dot(h, win_ref[:, aw:2 * aw]), gk_ref)
    v = _dot(h, win_ref[:, 2 * aw:3 * aw])

    k_ref[0] = k.astype(_BF16)
    for ib in range(nib):
        kmean_ref[0, 0, ib:ib + 1, :] = jnp.mean(
            k[ib * MOBA_BLOCK:(ib + 1) * MOBA_BLOCK], axis=0, keepdims=True)

    qT = q.T.astype(_BF16)
    vT = v.T.astype(_BF16)
    zeros = jnp.zeros((HEAD_DIM, MOBA_BLOCK), _BF16)
    pad_rows = lax.broadcasted_iota(jnp.int32, (V_ROWS - HEAD_DIM, MOBA_BLOCK), 0)
    ones_row = jnp.where(pad_rows == 0, 1.0, 0.0).astype(_BF16)
    for hd in range(N_HEADS):
        lo = (hd % HEADS_PER_VREG) * HEAD_DIM
        for ib in range(nib):
            cols = slice(ib * MOBA_BLOCK, (ib + 1) * MOBA_BLOCK)
            rows = slice(hd * HEAD_DIM, (hd + 1) * HEAD_DIM)
            qT_ref[0, hd, ib, lo:lo + HEAD_DIM, :] = qT[rows, cols]
            qT_ref[0, hd, ib, HEAD_DIM - lo:2 * HEAD_DIM - lo, :] = zeros
            vT_ref[0, hd, ib, 0:HEAD_DIM, :] = vT[rows, cols]
            vT_ref[0, hd, ib, HEAD_DIM:V_ROWS, :] = ones_row

    cw = CONV_WIDTH
    a = _dot(h, win_ref[:, 3 * aw:3 * aw + cw]) + glub_ref[:, 0:cw]
    g = _dot(h, win_ref[:, 3 * aw + cw:3 * aw + 2 * cw]) + glub_ref[:, cw:2 * cw]
    hg_ref[0] = a * jax.nn.sigmoid(g)


def _proj_call(x, g1, win, glub, gq, gk, gmat, cos2, sin2, layer, wout, wgate, wup, wdown):
    B, S, D = x.shape
    tm = TM_PROJ
    nib = tm // MOBA_BLOCK
    nb = S // MOBA_BLOCK
    nt = S // tm
    n_steps = B * nt
    bf16_rows = 16
    const = lambda b, t: (0, 0)

    def slab_specs(w, steps_per_slab):
        rows = w.shape[1] * steps_per_slab // n_steps
        assert rows * n_steps == w.shape[1] * steps_per_slab and rows % bf16_rows == 0
        return (pl.BlockSpec((1, rows, w.shape[2]), lambda b, t: (layer, (b * nt + t) // steps_per_slab, 0)),
                pl.BlockSpec((rows, w.shape[2]), lambda b, t: ((b * nt + t) // steps_per_slab, 0)))

    later = [slab_specs(wout, 1), slab_specs(wgate, 1), slab_specs(wup, 1), slab_specs(wdown, 2)]
    return pl.pallas_call(
        _proj_kernel,
        grid=(B, nt),
        in_specs=[
            pl.BlockSpec((1, tm, D), lambda b, t: (b, t, 0)),
            pl.BlockSpec((1, D), const),
            pl.BlockSpec((1, D, D_IN), lambda b, t: (layer, 0, 0), pipeline_mode=pl.Buffered(1)),
            pl.BlockSpec((1, 2 * CONV_WIDTH), const),
            pl.BlockSpec((1, ATTN_WIDTH), const),
            pl.BlockSpec((1, ATTN_WIDTH), const),
            pl.BlockSpec((ATTN_WIDTH, ATTN_WIDTH), const),
            pl.BlockSpec((tm, LANES), lambda b, t: (t, 0)),
            pl.BlockSpec((tm, LANES), lambda b, t: (t, 0)),
        ] + [spec_in for spec_in, _ in later],
        out_specs=[
            pl.BlockSpec((1, N_HEADS, nib, LANES, MOBA_BLOCK), lambda b, t: (b, 0, t, 0, 0)),
            pl.BlockSpec((1, tm, ATTN_WIDTH), lambda b, t: (b, t, 0)),
            pl.BlockSpec((1, N_HEADS, nib, V_ROWS, MOBA_BLOCK), lambda b, t: (b, 0, t, 0, 0)),
            pl.BlockSpec((1, 1, nib, ATTN_WIDTH), lambda b, t: (b, t, 0, 0)),
            pl.BlockSpec((1, tm, CONV_WIDTH), lambda b, t: (b, t, 0)),
        ] + [spec_out for _, spec_out in later],
        out_shape=[
            jax.ShapeDtypeStruct((B, N_HEADS, nb, LANES, MOBA_BLOCK), _BF16),
            jax.ShapeDtypeStruct((B, S, ATTN_WIDTH), _BF16),
            jax.ShapeDtypeStruct((B, N_HEADS, nb, V_ROWS, MOBA_BLOCK), _BF16),
            jax.ShapeDtypeStruct((B, S // tm, nib, ATTN_WIDTH), _F32),
            jax.ShapeDtypeStruct((B, S, CONV_WIDTH), _F32),
        ] + [jax.ShapeDtypeStruct(w.shape[1:], _BF16) for w in (wout, wgate, wup, wdown)],
        scratch_shapes=[pltpu.VMEM((D, D_IN), _BF16)],
        compiler_params=pltpu.CompilerParams(
            dimension_semantics=("arbitrary", "arbitrary"), vmem_limit_bytes=VMEM_LIMIT),
        name="moba_proj",
    )(x, g1, win, glub, gq, gk, gmat, cos2, sin2, wout, wgate, wup, wdown)


def _attn_kernel(qT_ref, k_ref, vT_ref, kmean_ref, o_ref,
                 acc_ref, m_ref, sel_ref, s_ref, smax_ref, p_ref):
    assert Q_BLOCKS == 2
    i0 = pl.program_id(1) * Q_BLOCKS
    nb = kmean_ref.shape[1]
    blk = MOBA_BLOCK
    own = nb
    row = lax.broadcasted_iota(jnp.int32, (blk, blk), 0)
    col = lax.broadcasted_iota(jnp.int32, (blk, blk), 1)
    causal = row <= col
    blk_id = lax.broadcasted_iota(jnp.int32, (nb, blk), 0).astype(_F32)
    chains = [(qb, hd) for qb in range(Q_BLOCKS) for hd in range(N_HEADS)]

    def k_block(j, pair):
        start = pl.multiple_of(j * blk, blk)
        return k_ref[0, pl.ds(start, blk), pair * LANES:(pair + 1) * LANES]

    def stage(c, j, masked):
        qb, hd = chains[c]
        s = _dot(k_block(j, hd // HEADS_PER_VREG), qT_ref[0, hd, qb])
        if masked:
            s = jnp.where(causal, s, NEG_BIG)
        s_ref[c] = s
        smax_ref[c] = jnp.max(s, axis=0, keepdims=True)

    def process(c, sel_row, j):
        qb, hd = chains[c]
        on = sel_ref[c, sel_row] > 0.5
        m_old = m_ref[c]
        m_new = jnp.maximum(m_old, jnp.where(on, smax_ref[c], NEG_BIG))
        alpha = jnp.exp2(m_old - m_new)
        shift = jnp.broadcast_to(jnp.where(on, m_new, POS_BIG), (EXP_ROWS, blk))
        for r in range(blk // EXP_ROWS):
            rows = slice(r * EXP_ROWS, (r + 1) * EXP_ROWS)
            p_ref[c, rows, :] = jnp.exp2(s_ref[c, rows, :] - shift).astype(_BF16)
        acc_ref[c] = alpha * acc_ref[c] + _dot(vT_ref[0, hd, j], p_ref[c])
        m_ref[c] = m_new

    for c, (qb, hd) in enumerate(chains):
        pair = hd // HEADS_PER_VREG
        qh = qT_ref[0, hd, qb]
        km = kmean_ref[0, :, pair * LANES:(pair + 1) * LANES].astype(_BF16)
        avail = blk_id < (i0 + qb).astype(_F32)
        gate = jnp.where(avail, _dot(km, qh), NEG_BIG)
        sel = jnp.zeros((nb, blk), _F32)
        for _ in range(MOBA_TOPK):
            best = jnp.max(gate, axis=0, keepdims=True)
            lowest = jnp.min(jnp.where(gate == best, blk_id, float(nb)), axis=0, keepdims=True)
            take = (blk_id == lowest) & avail
            sel = jnp.where(take, 1.0, sel)
            avail = avail & jnp.logical_not(take)
            gate = jnp.where(take, NEG_BIG, gate)
        for n in range(nb):
            sel_ref[c, n] = sel[n:n + 1, :]
        sel_ref[c, own] = jnp.ones((1, blk), _F32)
        m_ref[c] = jnp.full((1, blk), NEG_BIG, _F32)
        acc_ref[c] = jnp.zeros((V_ROWS, blk), _F32)
        stage(c, i0 + qb, masked=True)

    def sweep_step(u, carry):
        for c, (qb, hd) in enumerate(chains):
            process(c, jnp.where(u == 0, own, u - 1), jnp.where(u == 0, i0 + qb, u - 1))
            stage(c, u, masked=False)
        return carry

    lax.fori_loop(0, i0, sweep_step, 0)

    for c, (qb, hd) in enumerate(chains):
        process(c, jnp.where(i0 == 0, own, i0 - 1), jnp.where(i0 == 0, i0 + qb, i0 - 1))
        if qb == 1:
            stage(c, i0, masked=False)

    @pl.when(i0 >= 0)
    def _second_query_block_tail():
        for c, (qb, hd) in enumerate(chains):
            if qb == 1:
                process(c, i0, i0)

    for c, (qb, hd) in enumerate(chains):
        acc = acc_ref[c]
        denom = acc[HEAD_DIM:HEAD_DIM + 1, :]
        o_ref[0, hd, qb] = (acc[0:HEAD_DIM, :] * (1.0 / denom)).astype(_BF16)


def _attn_call(qT, k, vT, kmean):
    B, H, nb, _, blk = qT.shape
    S = k.shape[1]
    n_chains = Q_BLOCKS * H
    return pl.pallas_call(
        _attn_kernel,
        grid=(B, nb // Q_BLOCKS),
        in_specs=[
            pl.BlockSpec((1, H, Q_BLOCKS, LANES, blk), lambda b, g: (b, 0, g, 0, 0)),
            pl.BlockSpec((1, S, ATTN_WIDTH), lambda b, g: (b, 0, 0)),
            pl.BlockSpec((1, H, nb, V_ROWS, blk), lambda b, g: (b, 0, 0, 0, 0)),
            pl.BlockSpec((1, nb, ATTN_WIDTH), lambda b, g: (b, 0, 0)),
        ],
        out_specs=pl.BlockSpec((1, H, Q_BLOCKS, HEAD_DIM, blk), lambda b, g: (b, 0, g, 0, 0)),
        out_shape=jax.ShapeDtypeStruct((B, H, nb, HEAD_DIM, blk), _BF16),
        scratch_shapes=[
            pltpu.VMEM((n_chains, V_ROWS, blk), _F32),
            pltpu.VMEM((n_chains, 1, blk), _F32),
            pltpu.VMEM((n_chains, nb + 1, 1, blk), _F32),
            pltpu.VMEM((n_chains, blk, blk), _F32),
            pltpu.VMEM((n_chains, 1, blk), _F32),
            pltpu.VMEM((n_chains, blk, blk), _BF16),
        ],
        compiler_params=pltpu.CompilerParams(
            dimension_semantics=("arbitrary", "arbitrary"), vmem_limit_bytes=VMEM_LIMIT),
        name="moba_attn",
    )(qT, k, vT, kmean)


def _out_kernel(x_ref, oT_ref, hg0_ref, hgn_ref, halon_ref, dww_ref, dwb_ref, lng_ref, lnb_ref, wout_ref,
                g2_ref, wg_ref, wu_ref, wd_ref, y_ref, hbuf_ref, hshift_ref, wtap_ref, cact_ref,
                *, tiles_per_seq):
    tm = x_ref.shape[1]
    nib = tm // MOBA_BLOCK
    n = pl.program_id(0)

    n_chunks = tm // CONV_ROWS

    def exact_zero(v):
        return jnp.minimum(jnp.abs(v[0:SUBLANES, 0:LANES]), 0.0)

    def add_to_corner(v, z):
        top = jnp.concatenate([v[0:SUBLANES, 0:LANES] + z, v[0:SUBLANES, LANES:]], axis=1)
        return top if v.shape[0] == SUBLANES else jnp.concatenate([top, v[SUBLANES:]], axis=0)

    def conv_setup(hg_tile, halo):
        hbuf_ref[0:CONV_HALO, :] = halo
        hbuf_ref[CONV_HALO:CONV_HALO + tm, :] = hg_tile
        span = hshift_ref.shape[1]
        for r in range(1, SUBLANES):
            hshift_ref[r - 1] = hbuf_ref[r:r + span, :]
        for kk in range(CONV_KERNEL):
            wtap_ref[kk] = jnp.broadcast_to(dww_ref[kk:kk + 1, :], (SUBLANES, CONV_WIDTH))

    def conv_chunk(ci, after=None):
        c0 = ci * CONV_ROWS
        first = CONV_HALO - (CONV_KERNEL - 1)
        accs = [None] * (CONV_ROWS // SUBLANES)
        for kk in range(CONV_KERNEL):
            r = (first + kk) % SUBLANES
            lo = c0 + first + kk - r
            src = hbuf_ref if r == 0 else hshift_ref.at[r - 1]
            wk = wtap_ref[kk]
            for gi in range(len(accs)):
                term = wk * src[lo + gi * SUBLANES:lo + (gi + 1) * SUBLANES, :]
                if accs[gi] is None:
                    accs[gi] = term if after is None else add_to_corner(term, after)
                else:
                    accs[gi] = accs[gi] + term
        conv = jnp.concatenate(accs, axis=0) + dwb_ref[...]
        mu = jnp.mean(conv, axis=-1, keepdims=True)
        cen = conv - mu
        var = jnp.mean(cen * cen, axis=-1, keepdims=True)
        cn = cen * lax.rsqrt(var + EPS) * lng_ref[...] + lnb_ref[...]
        cact_ref[c0:c0 + CONV_ROWS, :] = _silu(cn).astype(_BF16)
        return exact_zero(cn)

    @pl.when(n == 0)
    def _first_tile():
        conv_setup(hg0_ref[0], jnp.zeros((CONV_HALO, CONV_WIDTH), _F32))
        for ci in range(n_chunks):
            conv_chunk(ci)

    cact = cact_ref[...]
    nxt = jnp.minimum(n + 1, pl.num_programs(0) - 1)
    halo = halon_ref[0]
    conv_setup(hgn_ref[0], jnp.where(nxt % tiles_per_seq == 0, jnp.zeros_like(halo), halo))

    finished = []

    def tied_dot(a, b):
        d = len(finished)
        out = _dot(a, b)
        if d < n_chunks:
            out = add_to_corner(out, conv_chunk(d, finished[d - CONV_CHAINS] if d >= CONV_CHAINS else None))
        finished.append(exact_zero(out))
        return out

    mix = _dot(cact, wout_ref[ATTN_WIDTH:ATTN_WIDTH + CONV_WIDTH, :])
    attn_parts = []
    for ib in range(nib):
        oT = oT_ref[0, :, ib].reshape(ATTN_WIDTH, MOBA_BLOCK)
        attn_parts.append(lax.dot_general(oT, wout_ref[0:ATTN_WIDTH, :], (((0,), (0,)), ((), ())),
                                          preferred_element_type=_F32))
    x1 = x_ref[0] + (mix + jnp.concatenate(attn_parts, axis=0))

    ms = jnp.mean(x1 * x1, axis=-1, keepdims=True)
    h2 = (x1 * lax.rsqrt(ms + EPS) * g2_ref[...]).astype(_BF16)
    y = x1
    for c0 in range(0, D_FF, FF_CHUNK):
        cs = slice(c0, c0 + FF_CHUNK)
        gt = tied_dot(h2, wg_ref[:, cs])
        up = tied_dot(h2, wu_ref[:, cs])
        act = (_silu(gt) * up).astype(_BF16)
        y = y + tied_dot(act, wd_ref[cs, :])
    assert len(finished) >= n_chunks, "every conv chunk must be tied to a matmul"
    y_ref[0] = y


def _out_call(x, oT, hg, dww, dwb, lng, lnb, wout, g2, wg, wu, wd):
    B, S, D = x.shape
    tm = TM_OUT
    nib = tm // MOBA_BLOCK
    H = N_HEADS
    const = lambda n: (0, 0)
    resident = functools.partial(pl.BlockSpec, pipeline_mode=pl.Buffered(1))
    halo_blocks = tm // CONV_HALO
    nt = S // tm
    n_steps = B * nt

    def next_tile(n):
        nxt = jnp.minimum(n + 1, n_steps - 1)
        return nxt // nt, nxt % nt

    def next_hg(n):
        b, t = next_tile(n)
        return b, t, 0

    def next_halo(n):
        b, t = next_tile(n)
        return b, jnp.maximum(t * halo_blocks - 1, 0), 0

    return pl.pallas_call(
        functools.partial(_out_kernel, tiles_per_seq=nt),
        grid=(n_steps,),
        in_specs=[
            pl.BlockSpec((1, tm, D), lambda n: (n // nt, n % nt, 0)),
            pl.BlockSpec((1, H, nib, HEAD_DIM, MOBA_BLOCK), lambda n: (n // nt, 0, n % nt, 0, 0)),
            pl.BlockSpec((1, tm, CONV_WIDTH), lambda n: (0, 0, 0)),
            pl.BlockSpec((1, tm, CONV_WIDTH), next_hg),
            pl.BlockSpec((1, CONV_HALO, CONV_WIDTH), next_halo),
            pl.BlockSpec((CONV_KERNEL, CONV_WIDTH), const),
            pl.BlockSpec((1, CONV_WIDTH), const),
            pl.BlockSpec((1, CONV_WIDTH), const),
            pl.BlockSpec((1, CONV_WIDTH), const),
            resident((D, D), const),
            pl.BlockSpec((1, D), const),
            resident((D, D_FF), const),
            resident((D, D_FF), const),
            resident((D_FF, D), const),
        ],
        out_specs=pl.BlockSpec((1, tm, D), lambda n: (n // nt, n % nt, 0)),
        out_shape=jax.ShapeDtypeStruct((B, S, D), _F32),
        scratch_shapes=[
            pltpu.VMEM((CONV_HALO + tm, CONV_WIDTH), _F32),
            pltpu.VMEM((SUBLANES - 1, CONV_HALO + tm - SUBLANES, CONV_WIDTH), _F32),
            pltpu.VMEM((CONV_KERNEL, SUBLANES, CONV_WIDTH), _F32),
            pltpu.VMEM((tm, CONV_WIDTH), _BF16),
        ],
        compiler_params=pltpu.CompilerParams(
            dimension_semantics=("arbitrary",), vmem_limit_bytes=VMEM_LIMIT),
        name="moba_out_ffn",
    )(x, oT, hg, hg, hg, dww, dwb, lng, lnb, wout, g2, wg, wu, wd)


def _rope_tables(seq_len):
    pos = jnp.arange(seq_len, dtype=_F32)
    inv_freq = ROPE_THETA ** (-jnp.arange(0, HEAD_DIM, 2, dtype=_F32) / HEAD_DIM)
    ang = pos[:, None] * inv_freq[None, :]
    ang = jnp.concatenate([ang, ang], axis=-1)
    sign = jnp.where(jnp.arange(HEAD_DIM) < HEAD_DIM // 2, -1.0, 1.0).astype(_F32)
    cos2 = jnp.tile(jnp.cos(ang), (1, HEADS_PER_VREG))
    sin2 = jnp.tile(jnp.sin(ang) * sign[None, :], (1, HEADS_PER_VREG))
    return cos2, sin2


def _layer(x, layer, norm1_g, w_in, glu_b, q_norm_g, k_norm_g, dw_w, dw_b, conv_ln_g, conv_ln_b,
           w_out, norm2_g, w_gate, w_up, w_down, cos2, sin2, gmat):
    B, S, _ = x.shape
    row = lambda a: a.reshape(1, -1)
    qT, k, vT, kmean, hg, w_out_bf, w_gate_bf, w_up_bf, w_down_bf = _proj_call(
        x, row(norm1_g), w_in, row(glu_b),
        row(jnp.tile(q_norm_g, N_HEADS)), row(jnp.tile(k_norm_g, N_HEADS)), gmat, cos2, sin2,
        layer, w_out, w_gate, w_up, w_down)
    kmean = kmean.reshape(B, S // MOBA_BLOCK, ATTN_WIDTH)
    oT = _attn_call(qT, k, vT, kmean)
    return _out_call(x, oT, hg, dw_w, row(dw_b), row(conv_ln_g), row(conv_ln_b),
                     w_out_bf, row(norm2_g), w_gate_bf, w_up_bf, w_down_bf)


def kernel(x, norm1_g, w_in, glu_b, q_norm_g, k_norm_g, dw_w, dw_b, conv_ln_g, conv_ln_b, w_out,
           norm2_g, w_gate, w_up, w_down):
    S = x.shape[1]
    cos2, sin2 = _rope_tables(S)
    head_of = jnp.arange(ATTN_WIDTH) // HEAD_DIM
    gmat = jnp.where(head_of[:, None] == head_of[None, :], 1.0 / HEAD_DIM, 0.0).astype(_BF16)
    for l in range(norm1_g.shape[0]):
        x = _layer(x, l, norm1_g[l], w_in, glu_b[l], q_norm_g[l], k_norm_g[l], dw_w[l], dw_b[l],
                   conv_ln_g[l], conv_ln_b[l], w_out, norm2_g[l], w_gate, w_up, w_down,
                   cos2, sin2, gmat)
    return x
```

```python
import functools

import jax
import jax.numpy as jnp
from jax import lax
from jax.experimental import pallas as pl
from jax.experimental.pallas import tpu as pltpu

D_MODEL = 1024
ATTN_WIDTH = 512
CONV_WIDTH = 512
N_HEADS = 8
HEAD_DIM = 64
CONV_KERNEL = 31
MOBA_BLOCK = 256
MOBA_TOPK = 3
ROPE_THETA = 10000.0
D_FF = 2816
EPS = 1e-6
D_IN = 3 * ATTN_WIDTH + 2 * CONV_WIDTH

LANES = 128
HEADS_PER_VREG = LANES // HEAD_DIM
V_ROWS = 80
Q_BLOCKS = 4
EXP_ROWS = 16
CONV_HALO = 32
LOG2_E = 1.4426950408889634
Q_SCALE = HEAD_DIM ** -0.5 * LOG2_E
NEG_BIG = -1e30
POS_BIG = 1e30

TM_PROJ = 512
TM_OUT = 512
FF_CHUNK = 256
SUBLANES = 8
CONV_ROWS = 16
CONV_CHAINS = 4
VMEM_LIMIT = 56 * 1024 * 1024

_BF16 = jnp.bfloat16
_F32 = jnp.float32


def _dot(a, b):
    return jnp.dot(a, b, preferred_element_type=_F32)


def _silu(x):
    half = 0.5 * x
    return half + half * jnp.tanh(half)


def _proj_kernel(x_ref, g1_ref, win_ref, glub_ref, gq_ref, gk_ref, gmat_ref, cos_ref, sin_ref,
                 wout_f32, wgate_f32, wup_f32, wdown_f32,
                 qT_ref, k_ref, vT_ref, kmean_ref, hg_ref, wout_bf, wgate_bf, wup_bf, wdown_bf,
                 winbf_ref):
    tm = x_ref.shape[1]
    nib = tm // MOBA_BLOCK

    step = pl.program_id(0) * pl.num_programs(1) + pl.program_id(1)

    @pl.when(step == 0)
    def _cast_w_in():
        for c0 in range(0, D_IN, ATTN_WIDTH):
            winbf_ref[:, c0:c0 + ATTN_WIDTH] = win_ref[0, :, c0:c0 + ATTN_WIDTH].astype(_BF16)

    @pl.when(step % 2 == 0)
    def _cast_w_down():
        wdown_bf[...] = wdown_f32[0].astype(_BF16)

    for src, dst in ((wout_f32, wout_bf), (wgate_f32, wgate_bf), (wup_f32, wup_bf)):
        dst[...] = src[0].astype(_BF16)
    win_ref = winbf_ref

    x = x_ref[0]
    ms = jnp.mean(x * x, axis=-1, keepdims=True)
    h = (x * lax.rsqrt(ms + EPS) * g1_ref[...]).astype(_BF16)

    cos = cos_ref[...]
    sin = sin_ref[...]
    lane = lax.broadcasted_iota(jnp.int32, (tm, LANES), 1)
    first_half = (lane & (HEAD_DIM // 2)) == 0

    def head_norm_rope(p, g_ref):
        msq = _dot((p * p).astype(_BF16), gmat_ref[...])
        pn = p * lax.rsqrt(msq + EPS) * g_ref[...]
        outs = []
        for c in range(ATTN_WIDTH // LANES):
            xc = pn[:, c * LANES:(c + 1) * LANES]
            partner = jnp.where(first_half,
                                pltpu.roll(xc, LANES - HEAD_DIM // 2, 1),
                                pltpu.roll(xc, HEAD_DIM // 2, 1))
            outs.append(xc * cos + partner * sin)
        return jnp.concatenate(outs, axis=1)

    aw = ATTN_WIDTH
    q = head_norm_rope(_dot(h, win_ref[:, 0:aw]), gq_ref) * Q_SCALE
    k = head_norm_rope(_dot(h, win_ref[:, aw:2 * aw]), gk_ref)
    v = _dot(h, win_ref[:, 2 * aw:3 * aw])

    k_ref[0] = k.astype(_BF16)
    for ib in range(nib):
        kmean_ref[0, 0, ib:ib + 1, :] = jnp.mean(
            k[ib * MOBA_BLOCK:(ib + 1) * MOBA_BLOCK], axis=0, keepdims=True)

    qT = q.T.astype(_BF16)
    vT = v.T.astype(_BF16)
    zeros = jnp.zeros((HEAD_DIM, MOBA_BLOCK), _BF16)
    pad_rows = lax.broadcasted_iota(jnp.int32, (V_ROWS - HEAD_DIM, MOBA_BLOCK), 0)
    ones_row = jnp.where(pad_rows == 0, 1.0, 0.0).astype(_BF16)
    for hd in range(N_HEADS):
        lo = (hd % HEADS_PER_VREG) * HEAD_DIM
        for ib in range(nib):
            cols = slice(ib * MOBA_BLOCK, (ib + 1) * MOBA_BLOCK)
            rows = slice(hd * HEAD_DIM, (hd + 1) * HEAD_DIM)
            qT_ref[0, hd, ib, lo:lo + HEAD_DIM, :] = qT[rows, cols]
            qT_ref[0, hd, ib, HEAD_DIM - lo:2 * HEAD_DIM - lo, :] = zeros
            vT_ref[0, hd, ib, 0:HEAD_DIM, :] = vT[rows, cols]
            vT_ref[0, hd, ib, HEAD_DIM:V_ROWS, :] = ones_row

    cw = CONV_WIDTH
    a = _dot(h, win_ref[:, 3 * aw:3 * aw + cw]) + glub_ref[:, 0:cw]
    g = _dot(h, win_ref[:, 3 * aw + cw:3 * aw + 2 * cw]) + glub_ref[:, cw:2 * cw]
    hg_ref[0] = a * jax.nn.sigmoid(g)


def _proj_call(x, g1, win, glub, gq, gk, gmat, cos2, sin2, layer, wout, wgate, wup, wdown):
    B, S, D = x.shape
    tm = TM_PROJ
    nib = tm // MOBA_BLOCK
    nb = S // MOBA_BLOCK
    nt = S // tm
    n_steps = B * nt
    bf16_rows = 16
    const = lambda b, t: (0, 0)

    def slab_specs(w, steps_per_slab):
        rows = w.shape[1] * steps_per_slab // n_steps
        assert rows * n_steps == w.shape[1] * steps_per_slab and rows % bf16_rows == 0
        return (pl.BlockSpec((1, rows, w.shape[2]), lambda b, t: (layer, (b * nt + t) // steps_per_slab, 0)),
                pl.BlockSpec((rows, w.shape[2]), lambda b, t: ((b * nt + t) // steps_per_slab, 0)))

    later = [slab_specs(wout, 1), slab_specs(wgate, 1), slab_specs(wup, 1), slab_specs(wdown, 2)]
    return pl.pallas_call(
        _proj_kernel,
        grid=(B, nt),
        in_specs=[
            pl.BlockSpec((1, tm, D), lambda b, t: (b, t, 0)),
            pl.BlockSpec((1, D), const),
            pl.BlockSpec((1, D, D_IN), lambda b, t: (layer, 0, 0), pipeline_mode=pl.Buffered(1)),
            pl.BlockSpec((1, 2 * CONV_WIDTH), const),
            pl.BlockSpec((1, ATTN_WIDTH), const),
            pl.BlockSpec((1, ATTN_WIDTH), const),
            pl.BlockSpec((ATTN_WIDTH, ATTN_WIDTH), const),
            pl.BlockSpec((tm, LANES), lambda b, t: (t, 0)),
            pl.BlockSpec((tm, LANES), lambda b, t: (t, 0)),
        ] + [spec_in for spec_in, _ in later],
        out_specs=[
            pl.BlockSpec((1, N_HEADS, nib, LANES, MOBA_BLOCK), lambda b, t: (b, 0, t, 0, 0)),
            pl.BlockSpec((1, tm, ATTN_WIDTH), lambda b, t: (b, t, 0)),
            pl.BlockSpec((1, N_HEADS, nib, V_ROWS, MOBA_BLOCK), lambda b, t: (b, 0, t, 0, 0)),
            pl.BlockSpec((1, 1, nib, ATTN_WIDTH), lambda b, t: (b, t, 0, 0)),
            pl.BlockSpec((1, tm, CONV_WIDTH), lambda b, t: (b, t, 0)),
        ] + [spec_out for _, spec_out in later],
        out_shape=[
            jax.ShapeDtypeStruct((B, N_HEADS, nb, LANES, MOBA_BLOCK), _BF16),
            jax.ShapeDtypeStruct((B, S, ATTN_WIDTH), _BF16),
            jax.ShapeDtypeStruct((B, N_HEADS, nb, V_ROWS, MOBA_BLOCK), _BF16),
            jax.ShapeDtypeStruct((B, S // tm, nib, ATTN_WIDTH), _F32),
            jax.ShapeDtypeStruct((B, S, CONV_WIDTH), _F32),
        ] + [jax.ShapeDtypeStruct(w.shape[1:], _BF16) for w in (wout, wgate, wup, wdown)],
        scratch_shapes=[pltpu.VMEM((D, D_IN), _BF16)],
        compiler_params=pltpu.CompilerParams(
            dimension_semantics=("arbitrary", "arbitrary"), vmem_limit_bytes=VMEM_LIMIT),
        name="moba_proj",
    )(x, g1, win, glub, gq, gk, gmat, cos2, sin2, wout, wgate, wup, wdown)


def _attn_kernel(qT_ref, k_ref, vT_ref, kmean_ref, o_ref,
                 acc_ref, m_ref, sel_ref, s_ref, smax_ref, p_ref):
    i0 = pl.program_id(1) * Q_BLOCKS
    nb = kmean_ref.shape[1]
    blk = MOBA_BLOCK
    own = nb
    row = lax.broadcasted_iota(jnp.int32, (blk, blk), 0)
    col = lax.broadcasted_iota(jnp.int32, (blk, blk), 1)
    causal = row <= col
    blk_id = lax.broadcasted_iota(jnp.int32, (nb, blk), 0).astype(_F32)
    chains = [(qb, hd) for qb in range(Q_BLOCKS) for hd in range(N_HEADS)]

    def k_block(j, pair):
        start = pl.multiple_of(j * blk, blk)
        return k_ref[0, pl.ds(start, blk), pair * LANES:(pair + 1) * LANES]

    def stage(c, j, masked):
        qb, hd = chains[c]
        s = _dot(k_block(j, hd // HEADS_PER_VREG), qT_ref[0, hd, qb])
        if masked:
            s = jnp.where(causal, s, NEG_BIG)
        s_ref[c] = s
        smax_ref[c] = jnp.max(s, axis=0, keepdims=True)

    def process(c, sel_row, j):
        qb, hd = chains[c]
        on = sel_ref[c, sel_row] > 0.5
        m_old = m_ref[c]
        m_new = jnp.maximum(m_old, jnp.where(on, smax_ref[c], NEG_BIG))
        alpha = jnp.exp2(m_old - m_new)
        shift = jnp.broadcast_to(jnp.where(on, m_new, POS_BIG), (EXP_ROWS, blk))
        for r in range(blk // EXP_ROWS):
            rows = slice(r * EXP_ROWS, (r + 1) * EXP_ROWS)
            p_ref[c, rows, :] = jnp.exp2(s_ref[c, rows, :] - shift).astype(_BF16)
        acc_ref[c] = alpha * acc_ref[c] + _dot(vT_ref[0, hd, j], p_ref[c])
        m_ref[c] = m_new

    for c, (qb, hd) in enumerate(chains):
        pair = hd // HEADS_PER_VREG
        qh = qT_ref[0, hd, qb]
        km = kmean_ref[0, :, pair * LANES:(pair + 1) * LANES].astype(_BF16)
        avail = blk_id < (i0 + qb).astype(_F32)
        gate = jnp.where(avail, _dot(km, qh), NEG_BIG)
        sel = jnp.zeros((nb, blk), _F32)
        for _ in range(MOBA_TOPK):
            best = jnp.max(gate, axis=0, keepdims=True)
            lowest = jnp.min(jnp.where(gate == best, blk_id, float(nb)), axis=0, keepdims=True)
            take = (blk_id == lowest) & avail
            sel = jnp.where(take, 1.0, sel)
            avail = avail & jnp.logical_not(take)
            gate = jnp.where(take, NEG_BIG, gate)
        for n in range(nb):
            sel_ref[c, n] = sel[n:n + 1, :]
        sel_ref[c, own] = jnp.ones((1, blk), _F32)
        m_ref[c] = jnp.full((1, blk), NEG_BIG, _F32)
        acc_ref[c] = jnp.zeros((V_ROWS, blk), _F32)
        stage(c, i0 + qb, masked=True)

    def sweep_step(u, carry):
        for c, (qb, hd) in enumerate(chains):
            process(c, jnp.where(u == 0, own, u - 1), jnp.where(u == 0, i0 + qb, u - 1))
            stage(c, u, masked=False)
        return carry

    lax.fori_loop(0, i0, sweep_step, 0)

    for c, (qb, hd) in enumerate(chains):
        process(c, jnp.where(i0 == 0, own, i0 - 1), jnp.where(i0 == 0, i0 + qb, i0 - 1))
        if qb >= 1:
            stage(c, i0, masked=False)

    for t in range(1, Q_BLOCKS):
        def tail_phase(t=t):
            for c, (qb, hd) in enumerate(chains):
                if qb >= t:
                    process(c, i0 + t - 1, i0 + t - 1)
                if qb >= t + 1:
                    stage(c, i0 + t, masked=False)
        pl.when(i0 + t > 0)(tail_phase)

    for c, (qb, hd) in enumerate(chains):
        acc = acc_ref[c]
        denom = acc[HEAD_DIM:HEAD_DIM + 1, :]
        o_ref[0, hd, qb] = (acc[0:HEAD_DIM, :] * (1.0 / denom)).astype(_BF16)


def _attn_call(qT, k, vT, kmean):
    B, H, nb, _, blk = qT.shape
    S = k.shape[1]
    n_chains = Q_BLOCKS * H
    return pl.pallas_call(
        _attn_kernel,
        grid=(B, nb // Q_BLOCKS),
        in_specs=[
            pl.BlockSpec((1, H, Q_BLOCKS, LANES, blk), lambda b, g: (b, 0, g, 0, 0)),
            pl.BlockSpec((1, S, ATTN_WIDTH), lambda b, g: (b, 0, 0)),
            pl.BlockSpec((1, H, nb, V_ROWS, blk), lambda b, g: (b, 0, 0, 0, 0)),
            pl.BlockSpec((1, nb, ATTN_WIDTH), lambda b, g: (b, 0, 0)),
        ],
        out_specs=pl.BlockSpec((1, H, Q_BLOCKS, HEAD_DIM, blk), lambda b, g: (b, 0, g, 0, 0)),
        out_shape=jax.ShapeDtypeStruct((B, H, nb, HEAD_DIM, blk), _BF16),
        scratch_shapes=[
            pltpu.VMEM((n_chains, V_ROWS, blk), _F32),
            pltpu.VMEM((n_chains, 1, blk), _F32),
            pltpu.VMEM((n_chains, nb + 1, 1, blk), _F32),
            pltpu.VMEM((n_chains, blk, blk), _F32),
            pltpu.VMEM((n_chains, 1, blk), _F32),
            pltpu.VMEM((n_chains, blk, blk), _BF16),
        ],
        compiler_params=pltpu.CompilerParams(
            dimension_semantics=("arbitrary", "arbitrary"), vmem_limit_bytes=VMEM_LIMIT),
        name="moba_attn",
    )(qT, k, vT, kmean)


def _out_kernel(x_ref, oT_ref, hg0_ref, hgn_ref, halon_ref, dww_ref, dwb_ref, lng_ref, lnb_ref, wout_ref,
                g2_ref, wg_ref, wu_ref, wd_ref, y_ref, hbuf_ref, hshift_ref, wtap_ref, cact_ref,
                *, tiles_per_seq):
    tm = x_ref.shape[1]
    nib = tm // MOBA_BLOCK
    n = pl.program_id(0)

    n_chunks = tm // CONV_ROWS

    def exact_zero(v):
        return jnp.minimum(jnp.abs(v[0:SUBLANES, 0:LANES]), 0.0)

    def add_to_corner(v, z):
        top = jnp.concatenate([v[0:SUBLANES, 0:LANES] + z, v[0:SUBLANES, LANES:]], axis=1)
        return top if v.shape[0] == SUBLANES else jnp.concatenate([top, v[SUBLANES:]], axis=0)

    def conv_setup(hg_tile, halo):
        hbuf_ref[0:CONV_HALO, :] = halo
        hbuf_ref[CONV_HALO:CONV_HALO + tm, :] = hg_tile
        span = hshift_ref.shape[1]
        for r in range(1, SUBLANES):
            hshift_ref[r - 1] = hbuf_ref[r:r + span, :]
        for kk in range(CONV_KERNEL):
            wtap_ref[kk] = jnp.broadcast_to(dww_ref[kk:kk + 1, :], (SUBLANES, CONV_WIDTH))

    def conv_chunk(ci, after=None):
        c0 = ci * CONV_ROWS
        first = CONV_HALO - (CONV_KERNEL - 1)
        accs = [None] * (CONV_ROWS // SUBLANES)
        for kk in range(CONV_KERNEL):
            r = (first + kk) % SUBLANES
            lo = c0 + first + kk - r
            src = hbuf_ref if r == 0 else hshift_ref.at[r - 1]
            wk = wtap_ref[kk]
            for gi in range(len(accs)):
                term = wk * src[lo + gi * SUBLANES:lo + (gi + 1) * SUBLANES, :]
                if accs[gi] is None:
                    accs[gi] = term if after is None else add_to_corner(term, after)
                else:
                    accs[gi] = accs[gi] + term
        conv = jnp.concatenate(accs, axis=0) + dwb_ref[...]
        mu = jnp.mean(conv, axis=-1, keepdims=True)
        cen = conv - mu
        var = jnp.mean(cen * cen, axis=-1, keepdims=True)
        cn = cen * lax.rsqrt(var + EPS) * lng_ref[...] + lnb_ref[...]
        cact_ref[c0:c0 + CONV_ROWS, :] = _silu(cn).astype(_BF16)
        return exact_zero(cn)

    @pl.when(n == 0)
    def _first_tile():
        conv_setup(hg0_ref[0], jnp.zeros((CONV_HALO, CONV_WIDTH), _F32))
        for ci in range(n_chunks):
            conv_chunk(ci)

    cact = cact_ref[...]
    nxt = jnp.minimum(n + 1, pl.num_programs(0) - 1)
    halo = halon_ref[0]
    conv_setup(hgn_ref[0], jnp.where(nxt % tiles_per_seq == 0, jnp.zeros_like(halo), halo))

    finished = []

    def tied_dot(a, b):
        d = len(finished)
        out = _dot(a, b)
        if d < n_chunks:
            out = add_to_corner(out, conv_chunk(d, finished[d - CONV_CHAINS] if d >= CONV_CHAINS else None))
        finished.append(exact_zero(out))
        return out

    mix = _dot(cact, wout_ref[ATTN_WIDTH:ATTN_WIDTH + CONV_WIDTH, :])
    attn_parts = []
    for ib in range(nib):
        oT = oT_ref[0, :, ib].reshape(ATTN_WIDTH, MOBA_BLOCK)
        attn_parts.append(lax.dot_general(oT, wout_ref[0:ATTN_WIDTH, :], (((0,), (0,)), ((), ())),
                                          preferred_element_type=_F32))
    x1 = x_ref[0] + (mix + jnp.concatenate(attn_parts, axis=0))

    ms = jnp.mean(x1 * x1, axis=-1, keepdims=True)
    h2 = (x1 * lax.rsqrt(ms + EPS) * g2_ref[...]).astype(_BF16)
    y = x1
    for c0 in range(0, D_FF, FF_CHUNK):
        cs = slice(c0, c0 + FF_CHUNK)
        gt = tied_dot(h2, wg_ref[:, cs])
        up = tied_dot(h2, wu_ref[:, cs])
        act = (_silu(gt) * up).astype(_BF16)
        y = y + tied_dot(act, wd_ref[cs, :])
    assert len(finished) >= n_chunks, "every conv chunk must be tied to a matmul"
    y_ref[0] = y


def _out_call(x, oT, hg, dww, dwb, lng, lnb, wout, g2, wg, wu, wd):
    B, S, D = x.shape
    tm = TM_OUT
    nib = tm // MOBA_BLOCK
    H = N_HEADS
    const = lambda n: (0, 0)
    resident = functools.partial(pl.BlockSpec, pipeline_mode=pl.Buffered(1))
    halo_blocks = tm // CONV_HALO
    nt = S // tm
    n_steps = B * nt

    def next_tile(n):
        nxt = jnp.minimum(n + 1, n_steps - 1)
        return nxt // nt, nxt % nt

    def next_hg(n):
        b, t = next_tile(n)
        return b, t, 0

    def next_halo(n):
        b, t = next_tile(n)
        return b, jnp.maximum(t * halo_blocks - 1, 0), 0

    return pl.pallas_call(
        functools.partial(_out_kernel, tiles_per_seq=nt),
        grid=(n_steps,),
        in_specs=[
            pl.BlockSpec((1, tm, D), lambda n: (n // nt, n % nt, 0)),
            pl.BlockSpec((1, H, nib, HEAD_DIM, MOBA_BLOCK), lambda n: (n // nt, 0, n % nt, 0, 0)),
            pl.BlockSpec((1, tm, CONV_WIDTH), lambda n: (0, 0, 0)),
            pl.BlockSpec((1, tm, CONV_WIDTH), next_hg),
            pl.BlockSpec((1, CONV_HALO, CONV_WIDTH), next_halo),
            pl.BlockSpec((CONV_KERNEL, CONV_WIDTH), const),
            pl.BlockSpec((1, CONV_WIDTH), const),
            pl.BlockSpec((1, CONV_WIDTH), const),
            pl.BlockSpec((1, CONV_WIDTH), const),
            resident((D, D), const),
            pl.BlockSpec((1, D), const),
            resident((D, D_FF), const),
            resident((D, D_FF), const),
            resident((D_FF, D), const),
        ],
        out_specs=pl.BlockSpec((1, tm, D), lambda n: (n // nt, n % nt, 0)),
        out_shape=jax.ShapeDtypeStruct((B, S, D), _F32),
        scratch_shapes=[
            pltpu.VMEM((CONV_HALO + tm, CONV_WIDTH), _F32),
            pltpu.VMEM((SUBLANES - 1, CONV_HALO + tm - SUBLANES, CONV_WIDTH), _F32),
            pltpu.VMEM((CONV_KERNEL, SUBLANES, CONV_WIDTH), _F32),
            pltpu.VMEM((tm, CONV_WIDTH), _BF16),
        ],
        compiler_params=pltpu.CompilerParams(
            dimension_semantics=("arbitrary",), vmem_limit_bytes=VMEM_LIMIT),
        name="moba_out_ffn",
    )(x, oT, hg, hg, hg, dww, dwb, lng, lnb, wout, g2, wg, wu, wd)


def _rope_tables(seq_len):
    pos = jnp.arange(seq_len, dtype=_F32)
    inv_freq = ROPE_THETA ** (-jnp.arange(0, HEAD_DIM, 2, dtype=_F32) / HEAD_DIM)
    ang = pos[:, None] * inv_freq[None, :]
    ang = jnp.concatenate([ang, ang], axis=-1)
    sign = jnp.where(jnp.arange(HEAD_DIM) < HEAD_DIM // 2, -1.0, 1.0).astype(_F32)
    cos2 = jnp.tile(jnp.cos(ang), (1, HEADS_PER_VREG))
    sin2 = jnp.tile(jnp.sin(ang) * sign[None, :], (1, HEADS_PER_VREG))
    return cos2, sin2


def _layer(x, layer, norm1_g, w_in, glu_b, q_norm_g, k_norm_g, dw_w, dw_b, conv_ln_g, conv_ln_b,
           w_out, norm2_g, w_gate, w_up, w_down, cos2, sin2, gmat):
    B, S, _ = x.shape
    row = lambda a: a.reshape(1, -1)
    qT, k, vT, kmean, hg, w_out_bf, w_gate_bf, w_up_bf, w_down_bf = _proj_call(
        x, row(norm1_g), w_in, row(glu_b),
        row(jnp.tile(q_norm_g, N_HEADS)), row(jnp.tile(k_norm_g, N_HEADS)), gmat, cos2, sin2,
        layer, w_out, w_gate, w_up, w_down)
    kmean = kmean.reshape(B, S // MOBA_BLOCK, ATTN_WIDTH)
    oT = _attn_call(qT, k, vT, kmean)
    return _out_call(x, oT, hg, dw_w, row(dw_b), row(conv_ln_g), row(conv_ln_b),
                     w_out_bf, row(norm2_g), w_gate_bf, w_up_bf, w_down_bf)


def kernel(x, norm1_g, w_in, glu_b, q_norm_g, k_norm_g, dw_w, dw_b, conv_ln_g, conv_ln_b, w_out,
           norm2_g, w_gate, w_up, w_down):
    S = x.shape[1]
    cos2, sin2 = _rope_tables(S)
    head_of = jnp.arange(ATTN_WIDTH) // HEAD_DIM
    gmat = jnp.where(head_of[:, None] == head_of[None, :], 1.0 / HEAD_DIM, 0.0).astype(_BF16)
    for l in range(norm1_g.shape[0]):
        x = _layer(x, l, norm1_g[l], w_in, glu_b[l], q_norm_g[l], k_norm_g[l], dw_w[l], dw_b[l],
                   conv_ln_g[l], conv_ln_b[l], w_out, norm2_g[l], w_gate, w_up, w_down,
                   cos2, sin2, gmat)
    return x
```

```python
import functools

import jax
import jax.numpy as jnp
from jax import lax
from jax.experimental import pallas as pl
from jax.experimental.pallas import tpu as pltpu

D_MODEL = 1024
ATTN_WIDTH = 512
CONV_WIDTH = 512
N_HEADS = 8
HEAD_DIM = 64
CONV_KERNEL = 31
MOBA_BLOCK = 256
MOBA_TOPK = 3
ROPE_THETA = 10000.0
D_FF = 2816
EPS = 1e-6
D_IN = 3 * ATTN_WIDTH + 2 * CONV_WIDTH

LANES = 128
HEADS_PER_VREG = LANES // HEAD_DIM
V_ROWS = 80
Q_BLOCKS = 4
EXP_ROWS = 16
CONV_HALO = 32
LOG2_E = 1.4426950408889634
Q_SCALE = HEAD_DIM ** -0.5 * LOG2_E
NEG_BIG = -1e30
POS_BIG = 1e30

TM_PROJ = 1024
TM_OUT = 512
FF_CHUNK = 256
SUBLANES = 8
CONV_ROWS = 16
CONV_CHAINS = 4
VMEM_LIMIT = 56 * 1024 * 1024

_BF16 = jnp.bfloat16
_F32 = jnp.float32


def _dot(a, b):
    return jnp.dot(a, b, preferred_element_type=_F32)


def _silu(x):
    half = 0.5 * x
    return half + half * jnp.tanh(half)


def _proj_kernel(x_ref, g1_ref, win_ref, glub_ref, gq_ref, gk_ref, gmat_ref, cos_ref, sin_ref,
                 wout_f32, wgate_f32, wup_f32, wdown_f32,
                 qT_ref, k_ref, vT_ref, kmean_ref, hg_ref, wout_bf, wgate_bf, wup_bf, wdown_bf,
                 winbf_ref):
    tm = x_ref.shape[1]
    nib = tm // MOBA_BLOCK

    step = pl.program_id(0) * pl.num_programs(1) + pl.program_id(1)

    @pl.when(step == 0)
    def _cast_w_in():
        for c0 in range(0, D_IN, ATTN_WIDTH):
            winbf_ref[:, c0:c0 + ATTN_WIDTH] = win_ref[0, :, c0:c0 + ATTN_WIDTH].astype(_BF16)

    @pl.when(step % 2 == 0)
    def _cast_w_down():
        wdown_bf[...] = wdown_f32[0].astype(_BF16)

    for src, dst in ((wout_f32, wout_bf), (wgate_f32, wgate_bf), (wup_f32, wup_bf)):
        dst[...] = src[0].astype(_BF16)
    win_ref = winbf_ref

    x = x_ref[0]
    ms = jnp.mean(x * x, axis=-1, keepdims=True)
    h = (x * lax.rsqrt(ms + EPS) * g1_ref[...]).astype(_BF16)

    cos = cos_ref[...]
    sin = sin_ref[...]
    lane = lax.broadcasted_iota(jnp.int32, (tm, LANES), 1)
    first_half = (lane & (HEAD_DIM // 2)) == 0

    def head_norm_rope(p, g_ref):
        msq = _dot((p * p).astype(_BF16), gmat_ref[...])
        pn = p * lax.rsqrt(msq + EPS) * g_ref[...]
        outs = []
        for c in range(ATTN_WIDTH // LANES):
            xc = pn[:, c * LANES:(c + 1) * LANES]
            partner = jnp.where(first_half,
                                pltpu.roll(xc, LANES - HEAD_DIM // 2, 1),
                                pltpu.roll(xc, HEAD_DIM // 2, 1))
            outs.append(xc * cos + partner * sin)
        return jnp.concatenate(outs, axis=1)

    aw = ATTN_WIDTH
    q = head_norm_rope(_dot(h, win_ref[:, 0:aw]), gq_ref) * Q_SCALE
    k = head_norm_rope(_dot(h, win_ref[:, aw:2 * aw]), gk_ref)
    v = _dot(h, win_ref[:, 2 * aw:3 * aw])

    k_ref[0] = k.astype(_BF16)
    for ib in range(nib):
        kmean_ref[0, 0, ib:ib + 1, :] = jnp.mean(
            k[ib * MOBA_BLOCK:(ib + 1) * MOBA_BLOCK], axis=0, keepdims=True)

    qT = q.T.astype(_BF16)
    vT = v.T.astype(_BF16)
    zeros = jnp.zeros((HEAD_DIM, MOBA_BLOCK), _BF16)
    pad_rows = lax.broadcasted_iota(jnp.int32, (V_ROWS - HEAD_DIM, MOBA_BLOCK), 0)
    ones_row = jnp.where(pad_rows == 0, 1.0, 0.0).astype(_BF16)
    for hd in range(N_HEADS):
        lo = (hd % HEADS_PER_VREG) * HEAD_DIM
        for ib in range(nib):
            cols = slice(ib * MOBA_BLOCK, (ib + 1) * MOBA_BLOCK)
            rows = slice(hd * HEAD_DIM, (hd + 1) * HEAD_DIM)
            qT_ref[0, hd, ib, lo:lo + HEAD_DIM, :] = qT[rows, cols]
            qT_ref[0, hd, ib, HEAD_DIM - lo:2 * HEAD_DIM - lo, :] = zeros
            vT_ref[0, hd, ib, 0:HEAD_DIM, :] = vT[rows, cols]
            vT_ref[0, hd, ib, HEAD_DIM:V_ROWS, :] = ones_row

    cw = CONV_WIDTH
    a = _dot(h, win_ref[:, 3 * aw:3 * aw + cw]) + glub_ref[:, 0:cw]
    g = _dot(h, win_ref[:, 3 * aw + cw:3 * aw + 2 * cw]) + glub_ref[:, cw:2 * cw]
    hg_ref[0] = a * jax.nn.sigmoid(g)


def _proj_call(x, g1, win, glub, gq, gk, gmat, cos2, sin2, layer, wout, wgate, wup, wdown):
    B, S, D = x.shape
    tm = TM_PROJ
    nib = tm // MOBA_BLOCK
    nb = S // MOBA_BLOCK
    nt = S // tm
    n_steps = B * nt
    bf16_rows = 16
    const = lambda b, t: (0, 0)

    def slab_specs(w, steps_per_slab):
        rows = w.shape[1] * steps_per_slab // n_steps
        assert rows * n_steps == w.shape[1] * steps_per_slab and rows % bf16_rows == 0
        return (pl.BlockSpec((1, rows, w.shape[2]), lambda b, t: (layer, (b * nt + t) // steps_per_slab, 0)),
                pl.BlockSpec((rows, w.shape[2]), lambda b, t: ((b * nt + t) // steps_per_slab, 0)))

    later = [slab_specs(wout, 1), slab_specs(wgate, 1), slab_specs(wup, 1), slab_specs(wdown, 2)]
    return pl.pallas_call(
        _proj_kernel,
        grid=(B, nt),
        in_specs=[
            pl.BlockSpec((1, tm, D), lambda b, t: (b, t, 0)),
            pl.BlockSpec((1, D), const),
            pl.BlockSpec((1, D, D_IN), lambda b, t: (layer, 0, 0), pipeline_mode=pl.Buffered(1)),
            pl.BlockSpec((1, 2 * CONV_WIDTH), const),
            pl.BlockSpec((1, ATTN_WIDTH), const),
            pl.BlockSpec((1, ATTN_WIDTH), const),
            pl.BlockSpec((ATTN_WIDTH, ATTN_WIDTH), const),
            pl.BlockSpec((tm, LANES), lambda b, t: (t, 0)),
            pl.BlockSpec((tm, LANES), lambda b, t: (t, 0)),
        ] + [spec_in for spec_in, _ in later],
        out_specs=[
            pl.BlockSpec((1, N_HEADS, nib, LANES, MOBA_BLOCK), lambda b, t: (b, 0, t, 0, 0)),
            pl.BlockSpec((1, tm, ATTN_WIDTH), lambda b, t: (b, t, 0)),
            pl.BlockSpec((1, N_HEADS, nib, V_ROWS, MOBA_BLOCK), lambda b, t: (b, 0, t, 0, 0)),
            pl.BlockSpec((1, 1, nib, ATTN_WIDTH), lambda b, t: (b, t, 0, 0)),
            pl.BlockSpec((1, tm, CONV_WIDTH), lambda b, t: (b, t, 0)),
        ] + [spec_out for _, spec_out in later],
        out_shape=[
            jax.ShapeDtypeStruct((B, N_HEADS, nb, LANES, MOBA_BLOCK), _BF16),
            jax.ShapeDtypeStruct((B, S, ATTN_WIDTH), _BF16),
            jax.ShapeDtypeStruct((B, N_HEADS, nb, V_ROWS, MOBA_BLOCK), _BF16),
            jax.ShapeDtypeStruct((B, S // tm, nib, ATTN_WIDTH), _F32),
            jax.ShapeDtypeStruct((B, S, CONV_WIDTH), _F32),
        ] + [jax.ShapeDtypeStruct(w.shape[1:], _BF16) for w in (wout, wgate, wup, wdown)],
        scratch_shapes=[pltpu.VMEM((D, D_IN), _BF16)],
        compiler_params=pltpu.CompilerParams(
            dimension_semantics=("arbitrary", "arbitrary"), vmem_limit_bytes=VMEM_LIMIT),
        name="moba_proj",
    )(x, g1, win, glub, gq, gk, gmat, cos2, sin2, wout, wgate, wup, wdown)


def _attn_kernel(qT_ref, k_ref, vT_ref, kmean_ref, o_ref,
                 acc_ref, m_ref, sel_ref, s_ref, smax_ref, p_ref):
    i0 = pl.program_id(1) * Q_BLOCKS
    nb = kmean_ref.shape[1]
    blk = MOBA_BLOCK
    own = nb
    row = lax.broadcasted_iota(jnp.int32, (blk, blk), 0)
    col = lax.broadcasted_iota(jnp.int32, (blk, blk), 1)
    causal = row <= col
    blk_id = lax.broadcasted_iota(jnp.int32, (nb, blk), 0).astype(_F32)
    chains = [(qb, hd) for qb in range(Q_BLOCKS) for hd in range(N_HEADS)]

    def k_block(j, pair):
        start = pl.multiple_of(j * blk, blk)
        return k_ref[0, pl.ds(start, blk), pair * LANES:(pair + 1) * LANES]

    def stage(c, j, masked):
        qb, hd = chains[c]
        s = _dot(k_block(j, hd // HEADS_PER_VREG), qT_ref[0, hd, qb])
        if masked:
            s = jnp.where(causal, s, NEG_BIG)
        s_ref[c] = s
        smax_ref[c] = jnp.max(s, axis=0, keepdims=True)

    def process(c, sel_row, j):
        qb, hd = chains[c]
        on = sel_ref[c, sel_row] > 0.5
        m_old = m_ref[c]
        m_new = jnp.maximum(m_old, jnp.where(on, smax_ref[c], NEG_BIG))
        alpha = jnp.exp2(m_old - m_new)
        shift = jnp.broadcast_to(jnp.where(on, m_new, POS_BIG), (EXP_ROWS, blk))
        for r in range(blk // EXP_ROWS):
            rows = slice(r * EXP_ROWS, (r + 1) * EXP_ROWS)
            p_ref[c, rows, :] = jnp.exp2(s_ref[c, rows, :] - shift).astype(_BF16)
        acc_ref[c] = alpha * acc_ref[c] + _dot(vT_ref[0, hd, j], p_ref[c])
        m_ref[c] = m_new

    for c, (qb, hd) in enumerate(chains):
        pair = hd // HEADS_PER_VREG
        qh = qT_ref[0, hd, qb]
        km = kmean_ref[0, :, pair * LANES:(pair + 1) * LANES].astype(_BF16)
        avail = blk_id < (i0 + qb).astype(_F32)
        gate = jnp.where(avail, _dot(km, qh), NEG_BIG)
        sel = jnp.zeros((nb, blk), _F32)
        for _ in range(MOBA_TOPK):
            best = jnp.max(gate, axis=0, keepdims=True)
            lowest = jnp.min(jnp.where(gate == best, blk_id, float(nb)), axis=0, keepdims=True)
            take = (blk_id == lowest) & avail
            sel = jnp.where(take, 1.0, sel)
            avail = avail & jnp.logical_not(take)
            gate = jnp.where(take, NEG_BIG, gate)
        for n in range(nb):
            sel_ref[c, n] = sel[n:n + 1, :]
        sel_ref[c, own] = jnp.ones((1, blk), _F32)
        m_ref[c] = jnp.full((1, blk), NEG_BIG, _F32)
        acc_ref[c] = jnp.zeros((V_ROWS, blk), _F32)
        stage(c, i0 + qb, masked=True)

    def sweep_step(u, carry):
        for c, (qb, hd) in enumerate(chains):
            process(c, jnp.where(u == 0, own, u - 1), jnp.where(u == 0, i0 + qb, u - 1))
            stage(c, u, masked=False)
        return carry

    lax.fori_loop(0, i0, sweep_step, 0)

    for c, (qb, hd) in enumerate(chains):
        process(c, jnp.where(i0 == 0, own, i0 - 1), jnp.where(i0 == 0, i0 + qb, i0 - 1))
        if qb >= 1:
            stage(c, i0, masked=False)

    for t in range(1, Q_BLOCKS):
        def tail_phase(t=t):
            for c, (qb, hd) in enumerate(chains):
                if qb >= t:
                    process(c, i0 + t - 1, i0 + t - 1)
                if qb >= t + 1:
                    stage(c, i0 + t, masked=False)
        pl.when(i0 + t > 0)(tail_phase)

    for c, (qb, hd) in enumerate(chains):
        acc = acc_ref[c]
        denom = acc[HEAD_DIM:HEAD_DIM + 1, :]
        o_ref[0, hd, qb] = (acc[0:HEAD_DIM, :] * (1.0 / denom)).astype(_BF16)


def _attn_call(qT, k, vT, kmean):
    B, H, nb, _, blk = qT.shape
    S = k.shape[1]
    n_chains = Q_BLOCKS * H
    return pl.pallas_call(
        _attn_kernel,
        grid=(B, nb // Q_BLOCKS),
        in_specs=[
            pl.BlockSpec((1, H, Q_BLOCKS, LANES, blk), lambda b, g: (b, 0, g, 0, 0)),
            pl.BlockSpec((1, S, ATTN_WIDTH), lambda b, g: (b, 0, 0)),
            pl.BlockSpec((1, H, nb, V_ROWS, blk), lambda b, g: (b, 0, 0, 0, 0)),
            pl.BlockSpec((1, nb, ATTN_WIDTH), lambda b, g: (b, 0, 0)),
        ],
        out_specs=pl.BlockSpec((1, H, Q_BLOCKS, HEAD_DIM, blk), lambda b, g: (b, 0, g, 0, 0)),
        out_shape=jax.ShapeDtypeStruct((B, H, nb, HEAD_DIM, blk), _BF16),
        scratch_shapes=[
            pltpu.VMEM((n_chains, V_ROWS, blk), _F32),
            pltpu.VMEM((n_chains, 1, blk), _F32),
            pltpu.VMEM((n_chains, nb + 1, 1, blk), _F32),
            pltpu.VMEM((n_chains, blk, blk), _F32),
            pltpu.VMEM((n_chains, 1, blk), _F32),
            pltpu.VMEM((n_chains, blk, blk), _BF16),
        ],
        compiler_params=pltpu.CompilerParams(
            dimension_semantics=("arbitrary", "arbitrary"), vmem_limit_bytes=VMEM_LIMIT),
        name="moba_attn",
    )(qT, k, vT, kmean)


def _out_kernel(x_ref, oT_ref, hg0_ref, hgn_ref, halon_ref, dww_ref, dwb_ref, lng_ref, lnb_ref, wout_ref,
                g2_ref, wg_ref, wu_ref, wd_ref, y_ref, hbuf_ref, hshift_ref, wtap_ref, cact_ref,
                *, tiles_per_seq):
    tm = x_ref.shape[1]
    nib = tm // MOBA_BLOCK
    n = pl.program_id(0)

    n_chunks = tm // CONV_ROWS

    def exact_zero(v):
        return jnp.minimum(jnp.abs(v[0:SUBLANES, 0:LANES]), 0.0)

    def add_to_corner(v, z):
        top = jnp.concatenate([v[0:SUBLANES, 0:LANES] + z, v[0:SUBLANES, LANES:]], axis=1)
        return top if v.shape[0] == SUBLANES else jnp.concatenate([top, v[SUBLANES:]], axis=0)

    def conv_setup(hg_tile, halo):
        hbuf_ref[0:CONV_HALO, :] = halo
        hbuf_ref[CONV_HALO:CONV_HALO + tm, :] = hg_tile
        span = hshift_ref.shape[1]
        for r in range(1, SUBLANES):
            hshift_ref[r - 1] = hbuf_ref[r:r + span, :]
        for kk in range(CONV_KERNEL):
            wtap_ref[kk] = jnp.broadcast_to(dww_ref[kk:kk + 1, :], (SUBLANES, CONV_WIDTH))

    def conv_chunk(ci, after=None):
        c0 = ci * CONV_ROWS
        first = CONV_HALO - (CONV_KERNEL - 1)
        accs = [None] * (CONV_ROWS // SUBLANES)
        for kk in range(CONV_KERNEL):
            r = (first + kk) % SUBLANES
            lo = c0 + first + kk - r
            src = hbuf_ref if r == 0 else hshift_ref.at[r - 1]
            wk = wtap_ref[kk]
            for gi in range(len(accs)):
                term = wk * src[lo + gi * SUBLANES:lo + (gi + 1) * SUBLANES, :]
                if accs[gi] is None:
                    accs[gi] = term if after is None else add_to_corner(term, after)
                else:
                    accs[gi] = accs[gi] + term
        conv = jnp.concatenate(accs, axis=0) + dwb_ref[...]
        mu = jnp.mean(conv, axis=-1, keepdims=True)
        cen = conv - mu
        var = jnp.mean(cen * cen, axis=-1, keepdims=True)
        cn = cen * lax.rsqrt(var + EPS) * lng_ref[...] + lnb_ref[...]
        cact_ref[c0:c0 + CONV_ROWS, :] = _silu(cn).astype(_BF16)
        return exact_zero(cn)

    @pl.when(n == 0)
    def _first_tile():
        conv_setup(hg0_ref[0], jnp.zeros((CONV_HALO, CONV_WIDTH), _F32))
        for ci in range(n_chunks):
            conv_chunk(ci)

    cact = cact_ref[...]
    nxt = jnp.minimum(n + 1, pl.num_programs(0) - 1)
    halo = halon_ref[0]
    conv_setup(hgn_ref[0], jnp.where(nxt % tiles_per_seq == 0, jnp.zeros_like(halo), halo))

    finished = []

    def tied_dot(a, b):
        d = len(finished)
        out = _dot(a, b)
        if d < n_chunks:
            out = add_to_corner(out, conv_chunk(d, finished[d - CONV_CHAINS] if d >= CONV_CHAINS else None))
        finished.append(exact_zero(out))
        return out

    mix = _dot(cact, wout_ref[ATTN_WIDTH:ATTN_WIDTH + CONV_WIDTH, :])
    attn_parts = []
    for ib in range(nib):
        oT = oT_ref[0, :, ib].reshape(ATTN_WIDTH, MOBA_BLOCK)
        attn_parts.append(lax.dot_general(oT, wout_ref[0:ATTN_WIDTH, :], (((0,), (0,)), ((), ())),
                                          preferred_element_type=_F32))
    x1 = x_ref[0] + (mix + jnp.concatenate(attn_parts, axis=0))

    ms = jnp.mean(x1 * x1, axis=-1, keepdims=True)
    h2 = (x1 * lax.rsqrt(ms + EPS) * g2_ref[...]).astype(_BF16)
    y = x1
    for c0 in range(0, D_FF, FF_CHUNK):
        cs = slice(c0, c0 + FF_CHUNK)
        gt = tied_dot(h2, wg_ref[:, cs])
        up = tied_dot(h2, wu_ref[:, cs])
        act = (_silu(gt) * up).astype(_BF16)
        y = y + tied_dot(act, wd_ref[cs, :])
    assert len(finished) >= n_chunks, "every conv chunk must be tied to a matmul"
    y_ref[0] = y


def _out_call(x, oT, hg, dww, dwb, lng, lnb, wout, g2, wg, wu, wd):
    B, S, D = x.shape
    tm = TM_OUT
    nib = tm // MOBA_BLOCK
    H = N_HEADS
    const = lambda n: (0, 0)
    resident = functools.partial(pl.BlockSpec, pipeline_mode=pl.Buffered(1))
    halo_blocks = tm // CONV_HALO
    nt = S // tm
    n_steps = B * nt

    def next_tile(n):
        nxt = jnp.minimum(n + 1, n_steps - 1)
        return nxt // nt, nxt % nt

    def next_hg(n):
        b, t = next_tile(n)
        return b, t, 0

    def next_halo(n):
        b, t = next_tile(n)
        return b, jnp.maximum(t * halo_blocks - 1, 0), 0

    return pl.pallas_call(
        functools.partial(_out_kernel, tiles_per_seq=nt),
        grid=(n_steps,),
        in_specs=[
            pl.BlockSpec((1, tm, D), lambda n: (n // nt, n % nt, 0)),
            pl.BlockSpec((1, H, nib, HEAD_DIM, MOBA_BLOCK), lambda n: (n // nt, 0, n % nt, 0, 0)),
            pl.BlockSpec((1, tm, CONV_WIDTH), lambda n: (0, 0, 0)),
            pl.BlockSpec((1, tm, CONV_WIDTH), next_hg),
            pl.BlockSpec((1, CONV_HALO, CONV_WIDTH), next_halo),
            pl.BlockSpec((CONV_KERNEL, CONV_WIDTH), const),
            pl.BlockSpec((1, CONV_WIDTH), const),
            pl.BlockSpec((1, CONV_WIDTH), const),
            pl.BlockSpec((1, CONV_WIDTH), const),
            resident((D, D), const),
            pl.BlockSpec((1, D), const),
            resident((D, D_FF), const),
            resident((D, D_FF), const),
            resident((D_FF, D), const),
        ],
        out_specs=pl.BlockSpec((1, tm, D), lambda n: (n // nt, n % nt, 0)),
        out_shape=jax.ShapeDtypeStruct((B, S, D), _F32),
        scratch_shapes=[
            pltpu.VMEM((CONV_HALO + tm, CONV_WIDTH), _F32),
            pltpu.VMEM((SUBLANES - 1, CONV_HALO + tm - SUBLANES, CONV_WIDTH), _F32),
            pltpu.VMEM((CONV_KERNEL, SUBLANES, CONV_WIDTH), _F32),
            pltpu.VMEM((tm, CONV_WIDTH), _BF16),
        ],
        compiler_params=pltpu.CompilerParams(
            dimension_semantics=("arbitrary",), vmem_limit_bytes=VMEM_LIMIT),
        name="moba_out_ffn",
    )(x, oT, hg, hg, hg, dww, dwb, lng, lnb, wout, g2, wg, wu, wd)


def _rope_tables(seq_len):
    pos = jnp.arange(seq_len, dtype=_F32)
    inv_freq = ROPE_THETA ** (-jnp.arange(0, HEAD_DIM, 2, dtype=_F32) / HEAD_DIM)
    ang = pos[:, None] * inv_freq[None, :]
    ang = jnp.concatenate([ang, ang], axis=-1)
    sign = jnp.where(jnp.arange(HEAD_DIM) < HEAD_DIM // 2, -1.0, 1.0).astype(_F32)
    cos2 = jnp.tile(jnp.cos(ang), (1, HEADS_PER_VREG))
    sin2 = jnp.tile(jnp.sin(ang) * sign[None, :], (1, HEADS_PER_VREG))
    return cos2, sin2


def _layer(x, layer, norm1_g, w_in, glu_b, q_norm_g, k_norm_g, dw_w, dw_b, conv_ln_g, conv_ln_b,
           w_out, norm2_g, w_gate, w_up, w_down, cos2, sin2, gmat):
    B, S, _ = x.shape
    row = lambda a: a.reshape(1, -1)
    qT, k, vT, kmean, hg, w_out_bf, w_gate_bf, w_up_bf, w_down_bf = _proj_call(
        x, row(norm1_g), w_in, row(glu_b),
        row(jnp.tile(q_norm_g, N_HEADS)), row(jnp.tile(k_norm_g, N_HEADS)), gmat, cos2, sin2,
        layer, w_out, w_gate, w_up, w_down)
    kmean = kmean.reshape(B, S // MOBA_BLOCK, ATTN_WIDTH)
    oT = _attn_call(qT, k, vT, kmean)
    return _out_call(x, oT, hg, dw_w, row(dw_b), row(conv_ln_g), row(conv_ln_b),
                     w_out_bf, row(norm2_g), w_gate_bf, w_up_bf, w_down_bf)


def kernel(x, norm1_g, w_in, glu_b, q_norm_g, k_norm_g, dw_w, dw_b, conv_ln_g, conv_ln_b, w_out,
           norm2_g, w_gate, w_up, w_down):
    S = x.shape[1]
    cos2, sin2 = _rope_tables(S)
    head_of = jnp.arange(ATTN_WIDTH) // HEAD_DIM
    gmat = jnp.where(head_of[:, None] == head_of[None, :], 1.0 / HEAD_DIM, 0.0).astype(_BF16)
    for l in range(norm1_g.shape[0]):
        x = _layer(x, l, norm1_g[l], w_in, glu_b[l], q_norm_g[l], k_norm_g[l], dw_w[l], dw_b[l],
                   conv_ln_g[l], conv_ln_b[l], w_out, norm2_g[l], w_gate, w_up, w_down,
                   cos2, sin2, gmat)
    return x
```

```python
import functools

import jax
import jax.numpy as jnp
from jax import lax
from jax.experimental import pallas as pl
from jax.experimental.pallas import tpu as pltpu

D_MODEL = 1024
ATTN_WIDTH = 512
CONV_WIDTH = 512
N_HEADS = 8
HEAD_DIM = 64
CONV_KERNEL = 31
MOBA_BLOCK = 256
MOBA_TOPK = 3
ROPE_THETA = 10000.0
D_FF = 2816
EPS = 1e-6
D_IN = 3 * ATTN_WIDTH + 2 * CONV_WIDTH

LANES = 128
HEADS_PER_VREG = LANES // HEAD_DIM
V_ROWS = 80
Q_BLOCKS = 4
EXP_ROWS = 16
CONV_HALO = 32
LOG2_E = 1.4426950408889634
Q_SCALE = HEAD_DIM ** -0.5 * LOG2_E
NEG_BIG = -1e30
POS_BIG = 1e30

TM_PROJ = 1024
TM_OUT = 512
FF_CHUNK = 256
SUBLANES = 8
CONV_ROWS = 16
CONV_CHAINS = 4
VMEM_LIMIT = 56 * 1024 * 1024

_BF16 = jnp.bfloat16
_F32 = jnp.float32


def _dot(a, b):
    return jnp.dot(a, b, preferred_element_type=_F32)


def _silu(x):
    half = 0.5 * x
    return half + half * jnp.tanh(half)


def _proj_kernel(x_ref, g1_ref, win_ref, glub_ref, gq_ref, gk_ref, gmat_ref, cos_ref, sin_ref,
                 qT_ref, k_ref, vT_ref, kmean_ref, hg_ref, winbf_ref):
    tm = x_ref.shape[1]
    nib = tm // MOBA_BLOCK

    @pl.when((pl.program_id(0) == 0) & (pl.program_id(1) == 0))
    def _cast_w_in():
        for c0 in range(0, D_IN, ATTN_WIDTH):
            winbf_ref[:, c0:c0 + ATTN_WIDTH] = win_ref[0, :, c0:c0 + ATTN_WIDTH].astype(_BF16)

    win_ref = winbf_ref

    x = x_ref[0]
    ms = jnp.mean(x * x, axis=-1, keepdims=True)
    h = (x * lax.rsqrt(ms + EPS) * g1_ref[...]).astype(_BF16)

    cos = cos_ref[...]
    sin = sin_ref[...]
    lane = lax.broadcasted_iota(jnp.int32, (tm, LANES), 1)
    first_half = (lane & (HEAD_DIM // 2)) == 0

    def head_norm_rope(p, g_ref):
        msq = _dot((p * p).astype(_BF16), gmat_ref[...])
        pn = p * lax.rsqrt(msq + EPS) * g_ref[...]
        outs = []
        for c in range(ATTN_WIDTH // LANES):
            xc = pn[:, c * LANES:(c + 1) * LANES]
            partner = jnp.where(first_half,
                                pltpu.roll(xc, LANES - HEAD_DIM // 2, 1),
                                pltpu.roll(xc, HEAD_DIM // 2, 1))
            outs.append(xc * cos + partner * sin)
        return jnp.concatenate(outs, axis=1)

    aw = ATTN_WIDTH
    q = head_norm_rope(_dot(h, win_ref[:, 0:aw]), gq_ref) * Q_SCALE
    k = head_norm_rope(_dot(h, win_ref[:, aw:2 * aw]), gk_ref)
    v = _dot(h, win_ref[:, 2 * aw:3 * aw])

    k_ref[0] = k.astype(_BF16)
    for ib in range(nib):
        kmean_ref[0, 0, ib:ib + 1, :] = jnp.mean(
            k[ib * MOBA_BLOCK:(ib + 1) * MOBA_BLOCK], axis=0, keepdims=True)

    qT = q.T.astype(_BF16)
    vT = v.T.astype(_BF16)
    zeros = jnp.zeros((HEAD_DIM, MOBA_BLOCK), _BF16)
    pad_rows = lax.broadcasted_iota(jnp.int32, (V_ROWS - HEAD_DIM, MOBA_BLOCK), 0)
    ones_row = jnp.where(pad_rows == 0, 1.0, 0.0).astype(_BF16)
    for hd in range(N_HEADS):
        lo = (hd % HEADS_PER_VREG) * HEAD_DIM
        for ib in range(nib):
            cols = slice(ib * MOBA_BLOCK, (ib + 1) * MOBA_BLOCK)
            rows = slice(hd * HEAD_DIM, (hd + 1) * HEAD_DIM)
            qT_ref[0, hd, ib, lo:lo + HEAD_DIM, :] = qT[rows, cols]
            qT_ref[0, hd, ib, HEAD_DIM - lo:2 * HEAD_DIM - lo, :] = zeros
            vT_ref[0, hd, ib, 0:HEAD_DIM, :] = vT[rows, cols]
            vT_ref[0, hd, ib, HEAD_DIM:V_ROWS, :] = ones_row

    cw = CONV_WIDTH
    a = _dot(h, win_ref[:, 3 * aw:3 * aw + cw]) + glub_ref[:, 0:cw]
    g = _dot(h, win_ref[:, 3 * aw + cw:3 * aw + 2 * cw]) + glub_ref[:, cw:2 * cw]
    hg_ref[0] = a * jax.nn.sigmoid(g)


def _proj_call(x, g1, win, glub, gq, gk, gmat, cos2, sin2, layer):
    B, S, D = x.shape
    tm = TM_PROJ
    nib = tm // MOBA_BLOCK
    nb = S // MOBA_BLOCK
    nt = S // tm
    const = lambda b, t: (0, 0)
    return pl.pallas_call(
        _proj_kernel,
        grid=(B, nt),
        in_specs=[
            pl.BlockSpec((1, tm, D), lambda b, t: (b, t, 0)),
            pl.BlockSpec((1, D), const),
            pl.BlockSpec((1, D, D_IN), lambda b, t: (layer, 0, 0), pipeline_mode=pl.Buffered(1)),
            pl.BlockSpec((1, 2 * CONV_WIDTH), const),
            pl.BlockSpec((1, ATTN_WIDTH), const),
            pl.BlockSpec((1, ATTN_WIDTH), const),
            pl.BlockSpec((ATTN_WIDTH, ATTN_WIDTH), const),
            pl.BlockSpec((tm, LANES), lambda b, t: (t, 0)),
            pl.BlockSpec((tm, LANES), lambda b, t: (t, 0)),
        ],
        out_specs=[
            pl.BlockSpec((1, N_HEADS, nib, LANES, MOBA_BLOCK), lambda b, t: (b, 0, t, 0, 0)),
            pl.BlockSpec((1, tm, ATTN_WIDTH), lambda b, t: (b, t, 0)),
            pl.BlockSpec((1, N_HEADS, nib, V_ROWS, MOBA_BLOCK), lambda b, t: (b, 0, t, 0, 0)),
            pl.BlockSpec((1, 1, nib, ATTN_WIDTH), lambda b, t: (b, t, 0, 0)),
            pl.BlockSpec((1, tm, CONV_WIDTH), lambda b, t: (b, t, 0)),
        ],
        out_shape=[
            jax.ShapeDtypeStruct((B, N_HEADS, nb, LANES, MOBA_BLOCK), _BF16),
            jax.ShapeDtypeStruct((B, S, ATTN_WIDTH), _BF16),
            jax.ShapeDtypeStruct((B, N_HEADS, nb, V_ROWS, MOBA_BLOCK), _BF16),
            jax.ShapeDtypeStruct((B, S // tm, nib, ATTN_WIDTH), _F32),
            jax.ShapeDtypeStruct((B, S, CONV_WIDTH), _F32),
        ],
        scratch_shapes=[pltpu.VMEM((D, D_IN), _BF16)],
        compiler_params=pltpu.CompilerParams(
            dimension_semantics=("arbitrary", "arbitrary"), vmem_limit_bytes=VMEM_LIMIT),
        name="moba_proj",
    )(x, g1, win, glub, gq, gk, gmat, cos2, sin2)


def _attn_kernel(qT_ref, k_ref, vT_ref, kmean_ref, wout_f32, wgate_f32, wup_f32, wdown_f32,
                 o_ref, wout_bf, wgate_bf, wup_bf, wdown_bf,
                 acc_ref, m_ref, sel_ref, s_ref, smax_ref, p_ref):
    for src, dst in ((wout_f32, wout_bf), (wgate_f32, wgate_bf), (wup_f32, wup_bf), (wdown_f32, wdown_bf)):
        dst[...] = src[0].astype(_BF16)

    i0 = pl.program_id(1) * Q_BLOCKS
    nb = kmean_ref.shape[1]
    blk = MOBA_BLOCK
    own = nb
    row = lax.broadcasted_iota(jnp.int32, (blk, blk), 0)
    col = lax.broadcasted_iota(jnp.int32, (blk, blk), 1)
    causal = row <= col
    blk_id = lax.broadcasted_iota(jnp.int32, (nb, blk), 0).astype(_F32)
    chains = [(qb, hd) for qb in range(Q_BLOCKS) for hd in range(N_HEADS)]

    def k_block(j, pair):
        start = pl.multiple_of(j * blk, blk)
        return k_ref[0, pl.ds(start, blk), pair * LANES:(pair + 1) * LANES]

    def stage(c, j, masked):
        qb, hd = chains[c]
        s = _dot(k_block(j, hd // HEADS_PER_VREG), qT_ref[0, hd, qb])
        if masked:
            s = jnp.where(causal, s, NEG_BIG)
        s_ref[c] = s
        smax_ref[c] = jnp.max(s, axis=0, keepdims=True)

    def process(c, sel_row, j):
        qb, hd = chains[c]
        on = sel_ref[c, sel_row] > 0.5
        m_old = m_ref[c]
        m_new = jnp.maximum(m_old, jnp.where(on, smax_ref[c], NEG_BIG))
        alpha = jnp.exp2(m_old - m_new)
        shift = jnp.broadcast_to(jnp.where(on, m_new, POS_BIG), (EXP_ROWS, blk))
        for r in range(blk // EXP_ROWS):
            rows = slice(r * EXP_ROWS, (r + 1) * EXP_ROWS)
            p_ref[c, rows, :] = jnp.exp2(s_ref[c, rows, :] - shift).astype(_BF16)
        acc_ref[c] = alpha * acc_ref[c] + _dot(vT_ref[0, hd, j], p_ref[c])
        m_ref[c] = m_new

    for c, (qb, hd) in enumerate(chains):
        pair = hd // HEADS_PER_VREG
        qh = qT_ref[0, hd, qb]
        km = kmean_ref[0, :, pair * LANES:(pair + 1) * LANES].astype(_BF16)
        avail = blk_id < (i0 + qb).astype(_F32)
        gate = jnp.where(avail, _dot(km, qh), NEG_BIG)
        sel = jnp.zeros((nb, blk), _F32)
        for _ in range(MOBA_TOPK):
            best = jnp.max(gate, axis=0, keepdims=True)
            lowest = jnp.min(jnp.where(gate == best, blk_id, float(nb)), axis=0, keepdims=True)
            take = (blk_id == lowest) & avail
            sel = jnp.where(take, 1.0, sel)
            avail = avail & jnp.logical_not(take)
            gate = jnp.where(take, NEG_BIG, gate)
        for n in range(nb):
            sel_ref[c, n] = sel[n:n + 1, :]
        sel_ref[c, own] = jnp.ones((1, blk), _F32)
        m_ref[c] = jnp.full((1, blk), NEG_BIG, _F32)
        acc_ref[c] = jnp.zeros((V_ROWS, blk), _F32)
        stage(c, i0 + qb, masked=True)

    def sweep_step(u, carry):
        for c, (qb, hd) in enumerate(chains):
            process(c, jnp.where(u == 0, own, u - 1), jnp.where(u == 0, i0 + qb, u - 1))
            stage(c, u, masked=False)
        return carry

    lax.fori_loop(0, i0, sweep_step, 0)

    for c, (qb, hd) in enumerate(chains):
        process(c, jnp.where(i0 == 0, own, i0 - 1), jnp.where(i0 == 0, i0 + qb, i0 - 1))
        if qb >= 1:
            stage(c, i0, masked=False)

    for t in range(1, Q_BLOCKS):
        def tail_phase(t=t):
            for c, (qb, hd) in enumerate(chains):
                if qb >= t:
                    process(c, i0 + t - 1, i0 + t - 1)
                if qb >= t + 1:
                    stage(c, i0 + t, masked=False)
        pl.when(i0 + t > 0)(tail_phase)

    for c, (qb, hd) in enumerate(chains):
        acc = acc_ref[c]
        denom = acc[HEAD_DIM:HEAD_DIM + 1, :]
        o_ref[0, hd, qb] = (acc[0:HEAD_DIM, :] * (1.0 / denom)).astype(_BF16)


def _attn_call(qT, k, vT, kmean, layer, later_weights):
    B, H, nb, _, blk = qT.shape
    S = k.shape[1]
    n_chains = Q_BLOCKS * H
    ng = nb // Q_BLOCKS
    n_steps = B * ng
    bf16_rows = 16

    def slab_specs(w):
        rows = w.shape[1] // n_steps
        assert rows * n_steps == w.shape[1] and rows % bf16_rows == 0
        return (pl.BlockSpec((1, rows, w.shape[2]), lambda b, g: (layer, b * ng + g, 0)),
                pl.BlockSpec((rows, w.shape[2]), lambda b, g: (b * ng + g, 0)))

    slabs = [slab_specs(w) for w in later_weights]
    return pl.pallas_call(
        _attn_kernel,
        grid=(B, ng),
        in_specs=[
            pl.BlockSpec((1, H, Q_BLOCKS, LANES, blk), lambda b, g: (b, 0, g, 0, 0)),
            pl.BlockSpec((1, S, ATTN_WIDTH), lambda b, g: (b, 0, 0)),
            pl.BlockSpec((1, H, nb, V_ROWS, blk), lambda b, g: (b, 0, 0, 0, 0)),
            pl.BlockSpec((1, nb, ATTN_WIDTH), lambda b, g: (b, 0, 0)),
        ] + [spec_in for spec_in, _ in slabs],
        out_specs=[pl.BlockSpec((1, H, Q_BLOCKS, HEAD_DIM, blk), lambda b, g: (b, 0, g, 0, 0))]
        + [spec_out for _, spec_out in slabs],
        out_shape=[jax.ShapeDtypeStruct((B, H, nb, HEAD_DIM, blk), _BF16)]
        + [jax.ShapeDtypeStruct(w.shape[1:], _BF16) for w in later_weights],
        scratch_shapes=[
            pltpu.VMEM((n_chains, V_ROWS, blk), _F32),
            pltpu.VMEM((n_chains, 1, blk), _F32),
            pltpu.VMEM((n_chains, nb + 1, 1, blk), _F32),
            pltpu.VMEM((n_chains, blk, blk), _F32),
            pltpu.VMEM((n_chains, 1, blk), _F32),
            pltpu.VMEM((n_chains, blk, blk), _BF16),
        ],
        compiler_params=pltpu.CompilerParams(
            dimension_semantics=("arbitrary", "arbitrary"), vmem_limit_bytes=VMEM_LIMIT),
        name="moba_attn",
    )(qT, k, vT, kmean, *later_weights)


def _out_kernel(x_ref, oT_ref, hg0_ref, hgn_ref, halon_ref, dww_ref, dwb_ref, lng_ref, lnb_ref, wout_ref,
                g2_ref, wg_ref, wu_ref, wd_ref, y_ref, hbuf_ref, hshift_ref, wtap_ref, cact_ref,
                *, tiles_per_seq):
    tm = x_ref.shape[1]
    nib = tm // MOBA_BLOCK
    n = pl.program_id(0)

    n_chunks = tm // CONV_ROWS

    def exact_zero(v):
        return jnp.minimum(jnp.abs(v[0:SUBLANES, 0:LANES]), 0.0)

    def add_to_corner(v, z):
        top = jnp.concatenate([v[0:SUBLANES, 0:LANES] + z, v[0:SUBLANES, LANES:]], axis=1)
        return top if v.shape[0] == SUBLANES else jnp.concatenate([top, v[SUBLANES:]], axis=0)

    def conv_setup(hg_tile, halo):
        hbuf_ref[0:CONV_HALO, :] = halo
        hbuf_ref[CONV_HALO:CONV_HALO + tm, :] = hg_tile
        span = hshift_ref.shape[1]
        for r in range(1, SUBLANES):
            hshift_ref[r - 1] = hbuf_ref[r:r + span, :]
        for kk in range(CONV_KERNEL):
            wtap_ref[kk] = jnp.broadcast_to(dww_ref[kk:kk + 1, :], (SUBLANES, CONV_WIDTH))

    def conv_chunk(ci, after=None):
        c0 = ci * CONV_ROWS
        first = CONV_HALO - (CONV_KERNEL - 1)
        accs = [None] * (CONV_ROWS // SUBLANES)
        for kk in range(CONV_KERNEL):
            r = (first + kk) % SUBLANES
            lo = c0 + first + kk - r
            src = hbuf_ref if r == 0 else hshift_ref.at[r - 1]
            wk = wtap_ref[kk]
            for gi in range(len(accs)):
                term = wk * src[lo + gi * SUBLANES:lo + (gi + 1) * SUBLANES, :]
                if accs[gi] is None:
                    accs[gi] = term if after is None else add_to_corner(term, after)
                else:
                    accs[gi] = accs[gi] + term
        conv = jnp.concatenate(accs, axis=0) + dwb_ref[...]
        mu = jnp.mean(conv, axis=-1, keepdims=True)
        cen = conv - mu
        var = jnp.mean(cen * cen, axis=-1, keepdims=True)
        cn = cen * lax.rsqrt(var + EPS) * lng_ref[...] + lnb_ref[...]
        cact_ref[c0:c0 + CONV_ROWS, :] = _silu(cn).astype(_BF16)
        return exact_zero(cn)

    @pl.when(n == 0)
    def _first_tile():
        conv_setup(hg0_ref[0], jnp.zeros((CONV_HALO, CONV_WIDTH), _F32))
        for ci in range(n_chunks):
            conv_chunk(ci)

    cact = cact_ref[...]
    nxt = jnp.minimum(n + 1, pl.num_programs(0) - 1)
    halo = halon_ref[0]
    conv_setup(hgn_ref[0], jnp.where(nxt % tiles_per_seq == 0, jnp.zeros_like(halo), halo))

    finished = []

    def tied_dot(a, b):
        d = len(finished)
        out = _dot(a, b)
        if d < n_chunks:
            out = add_to_corner(out, conv_chunk(d, finished[d - CONV_CHAINS] if d >= CONV_CHAINS else None))
        finished.append(exact_zero(out))
        return out

    mix = _dot(cact, wout_ref[ATTN_WIDTH:ATTN_WIDTH + CONV_WIDTH, :])
    attn_parts = []
    for ib in range(nib):
        oT = oT_ref[0, :, ib].reshape(ATTN_WIDTH, MOBA_BLOCK)
        attn_parts.append(lax.dot_general(oT, wout_ref[0:ATTN_WIDTH, :], (((0,), (0,)), ((), ())),
                                          preferred_element_type=_F32))
    x1 = x_ref[0] + (mix + jnp.concatenate(attn_parts, axis=0))

    ms = jnp.mean(x1 * x1, axis=-1, keepdims=True)
    h2 = (x1 * lax.rsqrt(ms + EPS) * g2_ref[...]).astype(_BF16)
    y = x1
    for c0 in range(0, D_FF, FF_CHUNK):
        cs = slice(c0, c0 + FF_CHUNK)
        gt = tied_dot(h2, wg_ref[:, cs])
        up = tied_dot(h2, wu_ref[:, cs])
        act = (_silu(gt) * up).astype(_BF16)
        y = y + tied_dot(act, wd_ref[cs, :])
    assert len(finished) >= n_chunks, "every conv chunk must be tied to a matmul"
    y_ref[0] = y


def _out_call(x, oT, hg, dww, dwb, lng, lnb, wout, g2, wg, wu, wd):
    B, S, D = x.shape
    tm = TM_OUT
    nib = tm // MOBA_BLOCK
    H = N_HEADS
    const = lambda n: (0, 0)
    resident = functools.partial(pl.BlockSpec, pipeline_mode=pl.Buffered(1))
    halo_blocks = tm // CONV_HALO
    nt = S // tm
    n_steps = B * nt

    def next_tile(n):
        nxt = jnp.minimum(n + 1, n_steps - 1)
        return nxt // nt, nxt % nt

    def next_hg(n):
        b, t = next_tile(n)
        return b, t, 0

    def next_halo(n):
        b, t = next_tile(n)
        return b, jnp.maximum(t * halo_blocks - 1, 0), 0

    return pl.pallas_call(
        functools.partial(_out_kernel, tiles_per_seq=nt),
        grid=(n_steps,),
        in_specs=[
            pl.BlockSpec((1, tm, D), lambda n: (n // nt, n % nt, 0)),
            pl.BlockSpec((1, H, nib, HEAD_DIM, MOBA_BLOCK), lambda n: (n // nt, 0, n % nt, 0, 0)),
            pl.BlockSpec((1, tm, CONV_WIDTH), lambda n: (0, 0, 0)),
            pl.BlockSpec((1, tm, CONV_WIDTH), next_hg),
            pl.BlockSpec((1, CONV_HALO, CONV_WIDTH), next_halo),
            pl.BlockSpec((CONV_KERNEL, CONV_WIDTH), const),
            pl.BlockSpec((1, CONV_WIDTH), const),
            pl.BlockSpec((1, CONV_WIDTH), const),
            pl.BlockSpec((1, CONV_WIDTH), const),
            resident((D, D), const),
            pl.BlockSpec((1, D), const),
            resident((D, D_FF), const),
            resident((D, D_FF), const),
            resident((D_FF, D), const),
        ],
        out_specs=pl.BlockSpec((1, tm, D), lambda n: (n // nt, n % nt, 0)),
        out_shape=jax.ShapeDtypeStruct((B, S, D), _F32),
        scratch_shapes=[
            pltpu.VMEM((CONV_HALO + tm, CONV_WIDTH), _F32),
            pltpu.VMEM((SUBLANES - 1, CONV_HALO + tm - SUBLANES, CONV_WIDTH), _F32),
            pltpu.VMEM((CONV_KERNEL, SUBLANES, CONV_WIDTH), _F32),
            pltpu.VMEM((tm, CONV_WIDTH), _BF16),
        ],
        compiler_params=pltpu.CompilerParams(
            dimension_semantics=("arbitrary",), vmem_limit_bytes=VMEM_LIMIT),
        name="moba_out_ffn",
    )(x, oT, hg, hg, hg, dww, dwb, lng, lnb, wout, g2, wg, wu, wd)


def _rope_tables(seq_len):
    pos = jnp.arange(seq_len, dtype=_F32)
    inv_freq = ROPE_THETA ** (-jnp.arange(0, HEAD_DIM, 2, dtype=_F32) / HEAD_DIM)
    ang = pos[:, None] * inv_freq[None, :]
    ang = jnp.concatenate([ang, ang], axis=-1)
    sign = jnp.where(jnp.arange(HEAD_DIM) < HEAD_DIM // 2, -1.0, 1.0).astype(_F32)
    cos2 = jnp.tile(jnp.cos(ang), (1, HEADS_PER_VREG))
    sin2 = jnp.tile(jnp.sin(ang) * sign[None, :], (1, HEADS_PER_VREG))
    return cos2, sin2


def _layer(x, layer, norm1_g, w_in, glu_b, q_norm_g, k_norm_g, dw_w, dw_b, conv_ln_g, conv_ln_b,
           w_out, norm2_g, w_gate, w_up, w_down, cos2, sin2, gmat):
    B, S, _ = x.shape
    row = lambda a: a.reshape(1, -1)
    qT, k, vT, kmean, hg = _proj_call(
        x, row(norm1_g), w_in, row(glu_b),
        row(jnp.tile(q_norm_g, N_HEADS)), row(jnp.tile(k_norm_g, N_HEADS)), gmat, cos2, sin2, layer)
    kmean = kmean.reshape(B, S // MOBA_BLOCK, ATTN_WIDTH)
    oT, w_out_bf, w_gate_bf, w_up_bf, w_down_bf = _attn_call(
        qT, k, vT, kmean, layer, [w_out, w_gate, w_up, w_down])
    return _out_call(x, oT, hg, dw_w, row(dw_b), row(conv_ln_g), row(conv_ln_b),
                     w_out_bf, row(norm2_g), w_gate_bf, w_up_bf, w_down_bf)


def kernel(x, norm1_g, w_in, glu_b, q_norm_g, k_norm_g, dw_w, dw_b, conv_ln_g, conv_ln_b, w_out,
           norm2_g, w_gate, w_up, w_down):
    S = x.shape[1]
    cos2, sin2 = _rope_tables(S)
    head_of = jnp.arange(ATTN_WIDTH) // HEAD_DIM
    gmat = jnp.where(head_of[:, None] == head_of[None, :], 1.0 / HEAD_DIM, 0.0).astype(_BF16)
    for l in range(norm1_g.shape[0]):
        x = _layer(x, l, norm1_g[l], w_in, glu_b[l], q_norm_g[l], k_norm_g[l], dw_w[l], dw_b[l],
                   conv_ln_g[l], conv_ln_b[l], w_out, norm2_g[l], w_gate, w_up, w_down,
                   cos2, sin2, gmat)
    return x
```

```python
import functools

import jax
import jax.numpy as jnp
import numpy as np
from jax import lax
from jax.experimental import pallas as pl
from jax.experimental.pallas import tpu as pltpu

D_MODEL = 1024
ATTN_WIDTH = 512
CONV_WIDTH = 512
N_HEADS = 8
HEAD_DIM = 64
CONV_KERNEL = 31
MOBA_BLOCK = 256
MOBA_TOPK = 3
ROPE_THETA = 10000.0
D_FF = 2816
EPS = 1e-6
D_IN = 3 * ATTN_WIDTH + 2 * CONV_WIDTH

LANES = 128
HEADS_PER_VREG = LANES // HEAD_DIM
V_ROWS = 80
Q_BLOCKS = 4
EXP_ROWS = 16
CONV_HALO = 32
LOG2_E = 1.4426950408889634
Q_SCALE = HEAD_DIM ** -0.5 * LOG2_E
NEG_BIG = -1e30
POS_BIG = 1e30

TM_PROJ = 1024
TM_OUT = 512
FF_CHUNK = 256
SUBLANES = 8
CONV_ROWS = 16
CONV_CHAINS = 4
VMEM_LIMIT = 56 * 1024 * 1024

_BF16 = jnp.bfloat16
_F32 = jnp.float32


def _dot(a, b):
    return jnp.dot(a, b, preferred_element_type=_F32)


def _silu(x):
    half = 0.5 * x
    return half + half * jnp.tanh(half)


def _proj_kernel(x_ref, g1_ref, win_ref, glub_ref, gq_ref, gk_ref, gmat_ref, cos_ref, sin_ref,
                 qT_ref, k_ref, vT_ref, kmean_ref, hg_ref, winbf_ref):
    tm = x_ref.shape[1]
    nib = tm // MOBA_BLOCK

    @pl.when((pl.program_id(0) == 0) & (pl.program_id(1) == 0))
    def _cast_w_in():
        for c0 in range(0, D_IN, ATTN_WIDTH):
            winbf_ref[:, c0:c0 + ATTN_WIDTH] = win_ref[0, :, c0:c0 + ATTN_WIDTH].astype(_BF16)

    win_ref = winbf_ref

    x = x_ref[0]
    ms = jnp.mean(x * x, axis=-1, keepdims=True)
    h = (x * lax.rsqrt(ms + EPS) * g1_ref[...]).astype(_BF16)

    cos = cos_ref[...]
    sin = sin_ref[...]
    lane = lax.broadcasted_iota(jnp.int32, (tm, LANES), 1)
    first_half = (lane & (HEAD_DIM // 2)) == 0

    def head_norm_rope(p, g_ref):
        msq = _dot((p * p).astype(_BF16), gmat_ref[...])
        pn = p * lax.rsqrt(msq + EPS) * g_ref[...]
        outs = []
        for c in range(ATTN_WIDTH // LANES):
            xc = pn[:, c * LANES:(c + 1) * LANES]
            partner = jnp.where(first_half,
                                pltpu.roll(xc, LANES - HEAD_DIM // 2, 1),
                                pltpu.roll(xc, HEAD_DIM // 2, 1))
            outs.append(xc * cos + partner * sin)
        return jnp.concatenate(outs, axis=1)

    aw = ATTN_WIDTH
    q = head_norm_rope(_dot(h, win_ref[:, 0:aw]), gq_ref) * Q_SCALE
    k = head_norm_rope(_dot(h, win_ref[:, aw:2 * aw]), gk_ref)
    v = _dot(h, win_ref[:, 2 * aw:3 * aw])

    k_ref[0] = k.astype(_BF16)
    for ib in range(nib):
        kmean_ref[0, 0, ib:ib + 1, :] = jnp.mean(
            k[ib * MOBA_BLOCK:(ib + 1) * MOBA_BLOCK], axis=0, keepdims=True)

    qT = q.T.astype(_BF16)
    vT = v.T.astype(_BF16)
    zeros = jnp.zeros((HEAD_DIM, MOBA_BLOCK), _BF16)
    pad_rows = lax.broadcasted_iota(jnp.int32, (V_ROWS - HEAD_DIM, MOBA_BLOCK), 0)
    ones_row = jnp.where(pad_rows == 0, 1.0, 0.0).astype(_BF16)
    for hd in range(N_HEADS):
        lo = (hd % HEADS_PER_VREG) * HEAD_DIM
        for ib in range(nib):
            cols = slice(ib * MOBA_BLOCK, (ib + 1) * MOBA_BLOCK)
            rows = slice(hd * HEAD_DIM, (hd + 1) * HEAD_DIM)
            qT_ref[0, hd, ib, lo:lo + HEAD_DIM, :] = qT[rows, cols]
            qT_ref[0, hd, ib, HEAD_DIM - lo:2 * HEAD_DIM - lo, :] = zeros
            vT_ref[0, hd, ib, 0:HEAD_DIM, :] = vT[rows, cols]
            vT_ref[0, hd, ib, HEAD_DIM:V_ROWS, :] = ones_row

    cw = CONV_WIDTH
    a = _dot(h, win_ref[:, 3 * aw:3 * aw + cw]) + glub_ref[:, 0:cw]
    g = _dot(h, win_ref[:, 3 * aw + cw:3 * aw + 2 * cw]) + glub_ref[:, cw:2 * cw]
    hg_ref[0] = a * jax.nn.sigmoid(g)


def _proj_call(x, g1, win, glub, gq, gk, gmat, cos2, sin2, layer):
    B, S, D = x.shape
    tm = TM_PROJ
    nib = tm // MOBA_BLOCK
    nb = S // MOBA_BLOCK
    nt = S // tm
    const = lambda b, t: (0, 0)
    return pl.pallas_call(
        _proj_kernel,
        grid=(B, nt),
        in_specs=[
            pl.BlockSpec((1, tm, D), lambda b, t: (b, t, 0)),
            pl.BlockSpec((1, D), const),
            pl.BlockSpec((1, D, D_IN), lambda b, t: (layer, 0, 0), pipeline_mode=pl.Buffered(1)),
            pl.BlockSpec((1, 2 * CONV_WIDTH), const),
            pl.BlockSpec((1, ATTN_WIDTH), const),
            pl.BlockSpec((1, ATTN_WIDTH), const),
            pl.BlockSpec((ATTN_WIDTH, ATTN_WIDTH), const),
            pl.BlockSpec((tm, LANES), lambda b, t: (t, 0)),
            pl.BlockSpec((tm, LANES), lambda b, t: (t, 0)),
        ],
        out_specs=[
            pl.BlockSpec((1, N_HEADS, nib, LANES, MOBA_BLOCK), lambda b, t: (b, 0, t, 0, 0)),
            pl.BlockSpec((1, tm, ATTN_WIDTH), lambda b, t: (b, t, 0)),
            pl.BlockSpec((1, N_HEADS, nib, V_ROWS, MOBA_BLOCK), lambda b, t: (b, 0, t, 0, 0)),
            pl.BlockSpec((1, 1, nib, ATTN_WIDTH), lambda b, t: (b, t, 0, 0)),
            pl.BlockSpec((1, tm, CONV_WIDTH), lambda b, t: (b, t, 0)),
        ],
        out_shape=[
            jax.ShapeDtypeStruct((B, N_HEADS, nb, LANES, MOBA_BLOCK), _BF16),
            jax.ShapeDtypeStruct((B, S, ATTN_WIDTH), _BF16),
            jax.ShapeDtypeStruct((B, N_HEADS, nb, V_ROWS, MOBA_BLOCK), _BF16),
            jax.ShapeDtypeStruct((B, S // tm, nib, ATTN_WIDTH), _F32),
            jax.ShapeDtypeStruct((B, S, CONV_WIDTH), _F32),
        ],
        scratch_shapes=[pltpu.VMEM((D, D_IN), _BF16)],
        compiler_params=pltpu.CompilerParams(
            dimension_semantics=("arbitrary", "arbitrary"), vmem_limit_bytes=VMEM_LIMIT),
        name="moba_proj",
    )(x, g1, win, glub, gq, gk, gmat, cos2, sin2)


def _attn_kernel(qT_ref, k_ref, vT_ref, kmean_ref, wout_f32, wgate_f32, wup_f32, wdown_f32,
                 o_ref, wout_bf, wgate_bf, wup_bf, wdown_bf,
                 acc_ref, m_ref, sel_ref, s_ref, smax_ref, p_ref):
    for src, dst in ((wout_f32, wout_bf), (wgate_f32, wgate_bf), (wup_f32, wup_bf), (wdown_f32, wdown_bf)):
        dst[...] = src[0].astype(_BF16)

    i0 = pl.program_id(1) * Q_BLOCKS
    nb = kmean_ref.shape[1]
    blk = MOBA_BLOCK
    own = nb
    row = lax.broadcasted_iota(jnp.int32, (blk, blk), 0)
    col = lax.broadcasted_iota(jnp.int32, (blk, blk), 1)
    causal = row <= col
    blk_id = lax.broadcasted_iota(jnp.int32, (nb, blk), 0).astype(_F32)
    chains = [(qb, hd) for qb in range(Q_BLOCKS) for hd in range(N_HEADS)]

    def k_block(j, pair):
        start = pl.multiple_of(j * blk, blk)
        return k_ref[0, pl.ds(start, blk), pair * LANES:(pair + 1) * LANES]

    def stage(c, j, masked):
        qb, hd = chains[c]
        s = _dot(k_block(j, hd // HEADS_PER_VREG), qT_ref[0, hd, qb])
        if masked:
            s = jnp.where(causal, s, NEG_BIG)
        s_ref[c] = s
        smax_ref[c] = jnp.max(s, axis=0, keepdims=True)

    def process(c, sel_row, j):
        qb, hd = chains[c]
        on = sel_ref[c, sel_row] > 0.5
        m_old = m_ref[c]
        m_new = jnp.maximum(m_old, jnp.where(on, smax_ref[c], NEG_BIG))
        alpha = jnp.exp2(m_old - m_new)
        shift = jnp.broadcast_to(jnp.where(on, m_new, POS_BIG), (EXP_ROWS, blk))
        for r in range(blk // EXP_ROWS):
            rows = slice(r * EXP_ROWS, (r + 1) * EXP_ROWS)
            p_ref[c, rows, :] = jnp.exp2(s_ref[c, rows, :] - shift).astype(_BF16)
        acc_ref[c] = alpha * acc_ref[c] + _dot(vT_ref[0, hd, j], p_ref[c])
        m_ref[c] = m_new

    for c, (qb, hd) in enumerate(chains):
        pair = hd // HEADS_PER_VREG
        qh = qT_ref[0, hd, qb]
        km = kmean_ref[0, :, pair * LANES:(pair + 1) * LANES].astype(_BF16)
        avail = blk_id < (i0 + qb).astype(_F32)
        gate = jnp.where(avail, _dot(km, qh), NEG_BIG)
        sel = jnp.zeros((nb, blk), _F32)
        for _ in range(MOBA_TOPK):
            best = jnp.max(gate, axis=0, keepdims=True)
            lowest = jnp.min(jnp.where(gate == best, blk_id, float(nb)), axis=0, keepdims=True)
            take = (blk_id == lowest) & avail
            sel = jnp.where(take, 1.0, sel)
            avail = avail & jnp.logical_not(take)
            gate = jnp.where(take, NEG_BIG, gate)
        for n in range(nb):
            sel_ref[c, n] = sel[n:n + 1, :]
        sel_ref[c, own] = jnp.ones((1, blk), _F32)
        m_ref[c] = jnp.full((1, blk), NEG_BIG, _F32)
        acc_ref[c] = jnp.zeros((V_ROWS, blk), _F32)
        stage(c, i0 + qb, masked=True)

    def sweep_step(u, carry):
        for c, (qb, hd) in enumerate(chains):
            process(c, jnp.where(u == 0, own, u - 1), jnp.where(u == 0, i0 + qb, u - 1))
            stage(c, u, masked=False)
        return carry

    lax.fori_loop(0, i0, sweep_step, 0)

    for c, (qb, hd) in enumerate(chains):
        process(c, jnp.where(i0 == 0, own, i0 - 1), jnp.where(i0 == 0, i0 + qb, i0 - 1))
        if qb >= 1:
            stage(c, i0, masked=False)

    for t in range(1, Q_BLOCKS):
        def tail_phase(t=t):
            for c, (qb, hd) in enumerate(chains):
                if qb >= t:
                    process(c, i0 + t - 1, i0 + t - 1)
                if qb >= t + 1:
                    stage(c, i0 + t, masked=False)
        pl.when(i0 + t > 0)(tail_phase)

    for c, (qb, hd) in enumerate(chains):
        acc = acc_ref[c]
        denom = acc[HEAD_DIM:HEAD_DIM + 1, :]
        o_ref[0, hd, qb] = (acc[0:HEAD_DIM, :] * (1.0 / denom)).astype(_BF16)


def _attn_call(qT, k, vT, kmean, layer, later_weights):
    B, H, nb, _, blk = qT.shape
    S = k.shape[1]
    n_chains = Q_BLOCKS * H
    ng = nb // Q_BLOCKS
    n_steps = B * ng
    bf16_rows = 16

    def slab_specs(w):
        rows = w.shape[1] // n_steps
        assert rows * n_steps == w.shape[1] and rows % bf16_rows == 0
        return (pl.BlockSpec((1, rows, w.shape[2]), lambda b, g: (layer, b * ng + g, 0)),
                pl.BlockSpec((rows, w.shape[2]), lambda b, g: (b * ng + g, 0)))

    slabs = [slab_specs(w) for w in later_weights]
    return pl.pallas_call(
        _attn_kernel,
        grid=(B, ng),
        in_specs=[
            pl.BlockSpec((1, H, Q_BLOCKS, LANES, blk), lambda b, g: (b, 0, g, 0, 0)),
            pl.BlockSpec((1, S, ATTN_WIDTH), lambda b, g: (b, 0, 0)),
            pl.BlockSpec((1, H, nb, V_ROWS, blk), lambda b, g: (b, 0, 0, 0, 0)),
            pl.BlockSpec((1, nb, ATTN_WIDTH), lambda b, g: (b, 0, 0)),
        ] + [spec_in for spec_in, _ in slabs],
        out_specs=[pl.BlockSpec((1, H, Q_BLOCKS, HEAD_DIM, blk), lambda b, g: (b, 0, g, 0, 0))]
        + [spec_out for _, spec_out in slabs],
        out_shape=[jax.ShapeDtypeStruct((B, H, nb, HEAD_DIM, blk), _BF16)]
        + [jax.ShapeDtypeStruct(w.shape[1:], _BF16) for w in later_weights],
        scratch_shapes=[
            pltpu.VMEM((n_chains, V_ROWS, blk), _F32),
            pltpu.VMEM((n_chains, 1, blk), _F32),
            pltpu.VMEM((n_chains, nb + 1, 1, blk), _F32),
            pltpu.VMEM((n_chains, blk, blk), _F32),
            pltpu.VMEM((n_chains, 1, blk), _F32),
            pltpu.VMEM((n_chains, blk, blk), _BF16),
        ],
        compiler_params=pltpu.CompilerParams(
            dimension_semantics=("arbitrary", "arbitrary"), vmem_limit_bytes=VMEM_LIMIT),
        name="moba_attn",
    )(qT, k, vT, kmean, *later_weights)


def _out_kernel(x_ref, oT_ref, hg0_ref, hgn_ref, halon_ref, dww_ref, dwb_ref, lng_ref, lnb_ref, wout_ref,
                g2_ref, wg_ref, wu_ref, wd_ref, y_ref, hbuf_ref, hshift_ref, wtap_ref, cact_ref,
                *, tiles_per_seq):
    tm = x_ref.shape[1]
    nib = tm // MOBA_BLOCK
    n = pl.program_id(0)

    n_chunks = tm // CONV_ROWS

    def exact_zero(v):
        return jnp.minimum(jnp.abs(v[0:SUBLANES, 0:LANES]), 0.0)

    def add_to_corner(v, z):
        top = jnp.concatenate([v[0:SUBLANES, 0:LANES] + z, v[0:SUBLANES, LANES:]], axis=1)
        return top if v.shape[0] == SUBLANES else jnp.concatenate([top, v[SUBLANES:]], axis=0)

    def conv_setup(hg_tile, halo):
        hbuf_ref[0:CONV_HALO, :] = halo
        hbuf_ref[CONV_HALO:CONV_HALO + tm, :] = hg_tile
        span = hshift_ref.shape[1]
        for r in range(1, SUBLANES):
            hshift_ref[r - 1] = hbuf_ref[r:r + span, :]
        for kk in range(CONV_KERNEL):
            wtap_ref[kk] = jnp.broadcast_to(dww_ref[kk:kk + 1, :], (SUBLANES, CONV_WIDTH))

    def conv_chunk(ci, after=None):
        c0 = ci * CONV_ROWS
        first = CONV_HALO - (CONV_KERNEL - 1)
        accs = [None] * (CONV_ROWS // SUBLANES)
        for kk in range(CONV_KERNEL):
            r = (first + kk) % SUBLANES
            lo = c0 + first + kk - r
            src = hbuf_ref if r == 0 else hshift_ref.at[r - 1]
            wk = wtap_ref[kk]
            for gi in range(len(accs)):
                term = wk * src[lo + gi * SUBLANES:lo + (gi + 1) * SUBLANES, :]
                if accs[gi] is None:
                    accs[gi] = term if after is None else add_to_corner(term, after)
                else:
                    accs[gi] = accs[gi] + term
        conv = jnp.concatenate(accs, axis=0) + dwb_ref[...]
        mu = jnp.mean(conv, axis=-1, keepdims=True)
        cen = conv - mu
        var = jnp.mean(cen * cen, axis=-1, keepdims=True)
        cn = cen * lax.rsqrt(var + EPS) * lng_ref[...] + lnb_ref[...]
        cact_ref[c0:c0 + CONV_ROWS, :] = _silu(cn).astype(_BF16)
        return exact_zero(cn)

    @pl.when(n == 0)
    def _first_tile():
        conv_setup(hg0_ref[0], jnp.zeros((CONV_HALO, CONV_WIDTH), _F32))
        for ci in range(n_chunks):
            conv_chunk(ci)

    cact = cact_ref[...]
    nxt = jnp.minimum(n + 1, pl.num_programs(0) - 1)
    halo = halon_ref[0]
    conv_setup(hgn_ref[0], jnp.where(nxt % tiles_per_seq == 0, jnp.zeros_like(halo), halo))

    finished = []

    def tied_dot(a, b):
        d = len(finished)
        out = _dot(a, b)
        if d < n_chunks:
            out = add_to_corner(out, conv_chunk(d, finished[d - CONV_CHAINS] if d >= CONV_CHAINS else None))
        finished.append(exact_zero(out))
        return out

    mix = _dot(cact, wout_ref[ATTN_WIDTH:ATTN_WIDTH + CONV_WIDTH, :])
    attn_parts = []
    for ib in range(nib):
        oT = oT_ref[0, :, ib].reshape(ATTN_WIDTH, MOBA_BLOCK)
        attn_parts.append(lax.dot_general(oT, wout_ref[0:ATTN_WIDTH, :], (((0,), (0,)), ((), ())),
                                          preferred_element_type=_F32))
    x1 = x_ref[0] + (mix + jnp.concatenate(attn_parts, axis=0))

    ms = jnp.mean(x1 * x1, axis=-1, keepdims=True)
    h2 = (x1 * lax.rsqrt(ms + EPS) * g2_ref[...]).astype(_BF16)
    y = x1
    for c0 in range(0, D_FF, FF_CHUNK):
        cs = slice(c0, c0 + FF_CHUNK)
        gt = tied_dot(h2, wg_ref[:, cs])
        up = tied_dot(h2, wu_ref[:, cs])
        act = (_silu(gt) * up).astype(_BF16)
        y = y + tied_dot(act, wd_ref[cs, :])
    assert len(finished) >= n_chunks, "every conv chunk must be tied to a matmul"
    y_ref[0] = y


def _out_call(x, oT, hg, dww, dwb, lng, lnb, wout, g2, wg, wu, wd):
    B, S, D = x.shape
    tm = TM_OUT
    nib = tm // MOBA_BLOCK
    H = N_HEADS
    const = lambda n: (0, 0)
    resident = functools.partial(pl.BlockSpec, pipeline_mode=pl.Buffered(1))
    halo_blocks = tm // CONV_HALO
    nt = S // tm
    n_steps = B * nt

    def next_tile(n):
        nxt = jnp.minimum(n + 1, n_steps - 1)
        return nxt // nt, nxt % nt

    def next_hg(n):
        b, t = next_tile(n)
        return b, t, 0

    def next_halo(n):
        b, t = next_tile(n)
        return b, jnp.maximum(t * halo_blocks - 1, 0), 0

    return pl.pallas_call(
        functools.partial(_out_kernel, tiles_per_seq=nt),
        grid=(n_steps,),
        in_specs=[
            pl.BlockSpec((1, tm, D), lambda n: (n // nt, n % nt, 0)),
            pl.BlockSpec((1, H, nib, HEAD_DIM, MOBA_BLOCK), lambda n: (n // nt, 0, n % nt, 0, 0)),
            pl.BlockSpec((1, tm, CONV_WIDTH), lambda n: (0, 0, 0)),
            pl.BlockSpec((1, tm, CONV_WIDTH), next_hg),
            pl.BlockSpec((1, CONV_HALO, CONV_WIDTH), next_halo),
            pl.BlockSpec((CONV_KERNEL, CONV_WIDTH), const),
            pl.BlockSpec((1, CONV_WIDTH), const),
            pl.BlockSpec((1, CONV_WIDTH), const),
            pl.BlockSpec((1, CONV_WIDTH), const),
            resident((D, D), const),
            pl.BlockSpec((1, D), const),
            resident((D, D_FF), const),
            resident((D, D_FF), const),
            resident((D_FF, D), const),
        ],
        out_specs=pl.BlockSpec((1, tm, D), lambda n: (n // nt, n % nt, 0)),
        out_shape=jax.ShapeDtypeStruct((B, S, D), _F32),
        scratch_shapes=[
            pltpu.VMEM((CONV_HALO + tm, CONV_WIDTH), _F32),
            pltpu.VMEM((SUBLANES - 1, CONV_HALO + tm - SUBLANES, CONV_WIDTH), _F32),
            pltpu.VMEM((CONV_KERNEL, SUBLANES, CONV_WIDTH), _F32),
            pltpu.VMEM((tm, CONV_WIDTH), _BF16),
        ],
        compiler_params=pltpu.CompilerParams(
            dimension_semantics=("arbitrary",), vmem_limit_bytes=VMEM_LIMIT),
        name="moba_out_ffn",
    )(x, oT, hg, hg, hg, dww, dwb, lng, lnb, wout, g2, wg, wu, wd)


def _rope_tables(seq_len):
    pos = np.arange(seq_len, dtype=np.float64)
    inv_freq = ROPE_THETA ** (-np.arange(0, HEAD_DIM, 2, dtype=np.float64) / HEAD_DIM)
    ang = pos[:, None] * inv_freq[None, :]
    ang = np.concatenate([ang, ang], axis=-1)
    sign = np.where(np.arange(HEAD_DIM) < HEAD_DIM // 2, -1.0, 1.0)
    cos2 = np.tile(np.cos(ang), (1, HEADS_PER_VREG))
    sin2 = np.tile(np.sin(ang) * sign[None, :], (1, HEADS_PER_VREG))
    return jnp.asarray(cos2, _F32), jnp.asarray(sin2, _F32)


def _layer(x, layer, norm1_g, w_in, glu_b, q_norm_g, k_norm_g, dw_w, dw_b, conv_ln_g, conv_ln_b,
           w_out, norm2_g, w_gate, w_up, w_down, cos2, sin2, gmat):
    B, S, _ = x.shape
    row = lambda a: a.reshape(1, -1)
    qT, k, vT, kmean, hg = _proj_call(
        x, row(norm1_g), w_in, row(glu_b),
        row(jnp.tile(q_norm_g, N_HEADS)), row(jnp.tile(k_norm_g, N_HEADS)), gmat, cos2, sin2, layer)
    kmean = kmean.reshape(B, S // MOBA_BLOCK, ATTN_WIDTH)
    oT, w_out_bf, w_gate_bf, w_up_bf, w_down_bf = _attn_call(
        qT, k, vT, kmean, layer, [w_out, w_gate, w_up, w_down])
    return _out_call(x, oT, hg, dw_w, row(dw_b), row(conv_ln_g), row(conv_ln_b),
                     w_out_bf, row(norm2_g), w_gate_bf, w_up_bf, w_down_bf)


def kernel(x, norm1_g, w_in, glu_b, q_norm_g, k_norm_g, dw_w, dw_b, conv_ln_g, conv_ln_b, w_out,
           norm2_g, w_gate, w_up, w_down):
    S = x.shape[1]
    cos2, sin2 = _rope_tables(S)
    head_of = np.arange(ATTN_WIDTH) // HEAD_DIM
    gmat = jnp.asarray(np.where(head_of[:, None] == head_of[None, :], 1.0 / HEAD_DIM, 0.0), _BF16)
    for l in range(norm1_g.shape[0]):
        x = _layer(x, l, norm1_g[l], w_in, glu_b[l], q_norm_g[l], k_norm_g[l], dw_w[l], dw_b[l],
                   conv_ln_g[l], conv_ln_b[l], w_out, norm2_g[l], w_gate, w_up, w_down,
                   cos2, sin2, gmat)
    return x
```

```python
import functools

import jax
import jax.numpy as jnp
import numpy as np
from jax import lax
from jax.experimental import pallas as pl
from jax.experimental.pallas import tpu as pltpu

D_MODEL = 1024
ATTN_WIDTH = 512
CONV_WIDTH = 512
N_HEADS = 8
HEAD_DIM = 64
CONV_KERNEL = 31
MOBA_BLOCK = 256
MOBA_TOPK = 3
ROPE_THETA = 10000.0
D_FF = 2816
EPS = 1e-6
D_IN = 3 * ATTN_WIDTH + 2 * CONV_WIDTH

LANES = 128
HEADS_PER_VREG = LANES // HEAD_DIM
V_ROWS = 80
Q_BLOCKS = 4
EXP_ROWS = 16
CONV_HALO = 32
LOG2_E = 1.4426950408889634
Q_SCALE = HEAD_DIM ** -0.5 * LOG2_E
NEG_BIG = -1e30
POS_BIG = 1e30

TM_PROJ = 1024
TM_OUT = 512
FF_CHUNK = 256
SUBLANES = 8
CONV_ROWS = 16
CONV_CHAINS = 4
VMEM_LIMIT = 56 * 1024 * 1024

_BF16 = jnp.bfloat16
_F32 = jnp.float32


def _dot(a, b):
    return jnp.dot(a, b, preferred_element_type=_F32)


def _silu(x):
    half = 0.5 * x
    return half + half * jnp.tanh(half)


def _proj_kernel(x_ref, g1_ref, win_ref, glub_ref, gq_ref, gk_ref, gmat_ref, cos_ref, sin_ref,
                 qT_ref, k_ref, vT_ref, kmean_ref, hg_ref, winbf_ref):
    tm = x_ref.shape[1]
    nib = tm // MOBA_BLOCK

    @pl.when((pl.program_id(0) == 0) & (pl.program_id(1) == 0))
    def _cast_w_in():
        for c0 in range(0, D_IN, ATTN_WIDTH):
            winbf_ref[:, c0:c0 + ATTN_WIDTH] = win_ref[0, :, c0:c0 + ATTN_WIDTH].astype(_BF16)

    win_ref = winbf_ref

    x = x_ref[0]
    ms = jnp.mean(x * x, axis=-1, keepdims=True)
    h = (x * lax.rsqrt(ms + EPS) * g1_ref[...]).astype(_BF16)

    cos = cos_ref[...]
    sin = sin_ref[...]
    lane = lax.broadcasted_iota(jnp.int32, (tm, LANES), 1)
    first_half = (lane & (HEAD_DIM // 2)) == 0

    def head_norm_rope(p, g_ref):
        msq = _dot((p * p).astype(_BF16), gmat_ref[...])
        pn = p * lax.rsqrt(msq + EPS) * g_ref[...]
        outs = []
        for c in range(ATTN_WIDTH // LANES):
            xc = pn[:, c * LANES:(c + 1) * LANES]
            partner = jnp.where(first_half,
                                pltpu.roll(xc, LANES - HEAD_DIM // 2, 1),
                                pltpu.roll(xc, HEAD_DIM // 2, 1))
            outs.append(xc * cos + partner * sin)
        return jnp.concatenate(outs, axis=1)

    aw = ATTN_WIDTH
    q = head_norm_rope(_dot(h, win_ref[:, 0:aw]), gq_ref) * Q_SCALE
    k = head_norm_rope(_dot(h, win_ref[:, aw:2 * aw]), gk_ref)
    v = _dot(h, win_ref[:, 2 * aw:3 * aw])

    k_ref[0] = k.astype(_BF16)
    for ib in range(nib):
        kmean_ref[0, 0, ib:ib + 1, :] = jnp.mean(
            k[ib * MOBA_BLOCK:(ib + 1) * MOBA_BLOCK], axis=0, keepdims=True)

    qT = q.T.astype(_BF16)
    vT = v.T.astype(_BF16)
    zeros = jnp.zeros((HEAD_DIM, MOBA_BLOCK), _BF16)
    pad_rows = lax.broadcasted_iota(jnp.int32, (V_ROWS - HEAD_DIM, MOBA_BLOCK), 0)
    ones_row = jnp.where(pad_rows == 0, 1.0, 0.0).astype(_BF16)
    for hd in range(N_HEADS):
        lo = (hd % HEADS_PER_VREG) * HEAD_DIM
        for ib in range(nib):
            cols = slice(ib * MOBA_BLOCK, (ib + 1) * MOBA_BLOCK)
            rows = slice(hd * HEAD_DIM, (hd + 1) * HEAD_DIM)
            qT_ref[0, hd, ib, lo:lo + HEAD_DIM, :] = qT[rows, cols]
            qT_ref[0, hd, ib, HEAD_DIM - lo:2 * HEAD_DIM - lo, :] = zeros
            vT_ref[0, hd, ib, 0:HEAD_DIM, :] = vT[rows, cols]
            vT_ref[0, hd, ib, HEAD_DIM:V_ROWS, :] = ones_row

    cw = CONV_WIDTH
    a = _dot(h, win_ref[:, 3 * aw:3 * aw + cw]) + glub_ref[:, 0:cw]
    g = _dot(h, win_ref[:, 3 * aw + cw:3 * aw + 2 * cw]) + glub_ref[:, cw:2 * cw]
    hg_ref[0] = a * jax.nn.sigmoid(g)


def _proj_call(x, g1, win, glub, gq, gk, gmat, cos2, sin2, layer):
    B, S, D = x.shape
    tm = TM_PROJ
    nib = tm // MOBA_BLOCK
    nb = S // MOBA_BLOCK
    nt = S // tm
    const = lambda b, t: (0, 0)
    return pl.pallas_call(
        _proj_kernel,
        grid=(B, nt),
        in_specs=[
            pl.BlockSpec((1, tm, D), lambda b, t: (b, t, 0)),
            pl.BlockSpec((1, D), const),
            pl.BlockSpec((1, D, D_IN), lambda b, t: (layer, 0, 0), pipeline_mode=pl.Buffered(1)),
            pl.BlockSpec((1, 2 * CONV_WIDTH), const),
            pl.BlockSpec((1, ATTN_WIDTH), const),
            pl.BlockSpec((1, ATTN_WIDTH), const),
            pl.BlockSpec((ATTN_WIDTH, ATTN_WIDTH), const),
            pl.BlockSpec((tm, LANES), lambda b, t: (t, 0)),
            pl.BlockSpec((tm, LANES), lambda b, t: (t, 0)),
        ],
        out_specs=[
            pl.BlockSpec((1, N_HEADS, nib, LANES, MOBA_BLOCK), lambda b, t: (b, 0, t, 0, 0)),
            pl.BlockSpec((1, tm, ATTN_WIDTH), lambda b, t: (b, t, 0)),
            pl.BlockSpec((1, N_HEADS, nib, V_ROWS, MOBA_BLOCK), lambda b, t: (b, 0, t, 0, 0)),
            pl.BlockSpec((1, 1, nib, ATTN_WIDTH), lambda b, t: (b, t, 0, 0)),
            pl.BlockSpec((1, tm, CONV_WIDTH), lambda b, t: (b, t, 0)),
        ],
        out_shape=[
            jax.ShapeDtypeStruct((B, N_HEADS, nb, LANES, MOBA_BLOCK), _BF16),
            jax.ShapeDtypeStruct((B, S, ATTN_WIDTH), _BF16),
            jax.ShapeDtypeStruct((B, N_HEADS, nb, V_ROWS, MOBA_BLOCK), _BF16),
            jax.ShapeDtypeStruct((B, S // tm, nib, ATTN_WIDTH), _F32),
            jax.ShapeDtypeStruct((B, S, CONV_WIDTH), _F32),
        ],
        scratch_shapes=[pltpu.VMEM((D, D_IN), _BF16)],
        compiler_params=pltpu.CompilerParams(
            dimension_semantics=("arbitrary", "arbitrary"), vmem_limit_bytes=VMEM_LIMIT),
        name="moba_proj",
    )(x, g1, win, glub, gq, gk, gmat, cos2, sin2)


def _attn_kernel(qT_ref, k_ref, vT_ref, kmean_ref, wout_f32, wgate_f32, wup_f32, wdown_f32,
                 o_ref, wout_bf, wgate_bf, wup_bf, wdown_bf,
                 acc_ref, m_ref, sel_ref, s_ref, smax_ref, p_ref):
    for src, dst in ((wout_f32, wout_bf), (wgate_f32, wgate_bf), (wup_f32, wup_bf), (wdown_f32, wdown_bf)):
        dst[...] = src[0].astype(_BF16)

    i0 = pl.program_id(1) * Q_BLOCKS
    nb = kmean_ref.shape[1]
    blk = MOBA_BLOCK
    own = nb
    row = lax.broadcasted_iota(jnp.int32, (blk, blk), 0)
    col = lax.broadcasted_iota(jnp.int32, (blk, blk), 1)
    causal = row <= col
    blk_id = lax.broadcasted_iota(jnp.int32, (nb, blk), 0).astype(_F32)
    chains = [(qb, hd) for qb in range(Q_BLOCKS) for hd in range(N_HEADS)]

    def k_block(j, pair):
        start = pl.multiple_of(j * blk, blk)
        return k_ref[0, pl.ds(start, blk), pair * LANES:(pair + 1) * LANES]

    def stage(c, j, masked):
        qb, hd = chains[c]
        s = _dot(k_block(j, hd // HEADS_PER_VREG), qT_ref[0, hd, qb])
        if masked:
            s = jnp.where(causal, s, NEG_BIG)
        s_ref[c] = s
        smax_ref[c] = jnp.max(s, axis=0, keepdims=True)

    def process(c, sel_row, j):
        qb, hd = chains[c]
        on = sel_ref[c, sel_row] > 0.5
        m_old = m_ref[c]
        m_new = jnp.maximum(m_old, jnp.where(on, smax_ref[c], NEG_BIG))
        alpha = jnp.exp2(m_old - m_new)
        shift = jnp.broadcast_to(jnp.where(on, m_new, POS_BIG), (EXP_ROWS, blk))
        for r in range(blk // EXP_ROWS):
            rows = slice(r * EXP_ROWS, (r + 1) * EXP_ROWS)
            p_ref[c, rows, :] = jnp.exp2(s_ref[c, rows, :] - shift).astype(_BF16)
        acc_ref[c] = alpha * acc_ref[c] + _dot(vT_ref[0, hd, j], p_ref[c])
        m_ref[c] = m_new

    for c, (qb, hd) in enumerate(chains):
        pair = hd // HEADS_PER_VREG
        qh = qT_ref[0, hd, qb]
        km = kmean_ref[0, :, pair * LANES:(pair + 1) * LANES].astype(_BF16)
        avail = blk_id < (i0 + qb).astype(_F32)
        gate = jnp.where(avail, _dot(km, qh), NEG_BIG)
        sel = jnp.zeros((nb, blk), _F32)
        for _ in range(MOBA_TOPK):
            best = jnp.max(gate, axis=0, keepdims=True)
            lowest = jnp.min(jnp.where(gate == best, blk_id, float(nb)), axis=0, keepdims=True)
            take = (blk_id == lowest) & avail
            sel = jnp.where(take, 1.0, sel)
            avail = avail & jnp.logical_not(take)
            gate = jnp.where(take, NEG_BIG, gate)
        for n in range(nb):
            sel_ref[c, n] = sel[n:n + 1, :]
        sel_ref[c, own] = jnp.ones((1, blk), _F32)
        m_ref[c] = jnp.full((1, blk), NEG_BIG, _F32)
        acc_ref[c] = jnp.zeros((V_ROWS, blk), _F32)
        stage(c, i0 + qb, masked=True)

    def sweep_step(u, carry):
        for c, (qb, hd) in enumerate(chains):
            process(c, jnp.where(u == 0, own, u - 1), jnp.where(u == 0, i0 + qb, u - 1))
            stage(c, u, masked=False)
        return carry

    lax.fori_loop(0, i0, sweep_step, 0)

    for c, (qb, hd) in enumerate(chains):
        process(c, jnp.where(i0 == 0, own, i0 - 1), jnp.where(i0 == 0, i0 + qb, i0 - 1))
        if qb >= 1:
            stage(c, i0, masked=False)

    for t in range(1, Q_BLOCKS):
        def tail_phase(t=t):
            for c, (qb, hd) in enumerate(chains):
                if qb >= t:
                    process(c, i0 + t - 1, i0 + t - 1)
                if qb >= t + 1:
                    stage(c, i0 + t, masked=False)
        pl.when(i0 + t > 0)(tail_phase)

    for c, (qb, hd) in enumerate(chains):
        acc = acc_ref[c]
        denom = acc[HEAD_DIM:HEAD_DIM + 1, :]
        o_ref[0, hd, qb] = (acc[0:HEAD_DIM, :] * (1.0 / denom)).astype(_BF16)


def _attn_call(qT, k, vT, kmean, layer, later_weights):
    B, H, nb, _, blk = qT.shape
    S = k.shape[1]
    n_chains = Q_BLOCKS * H
    ng = nb // Q_BLOCKS
    n_steps = B * ng
    bf16_rows = 16

    def slab_specs(w):
        rows = w.shape[1] // n_steps
        assert rows * n_steps == w.shape[1] and rows % bf16_rows == 0
        return (pl.BlockSpec((1, rows, w.shape[2]), lambda b, g: (layer, b * ng + g, 0)),
                pl.BlockSpec((rows, w.shape[2]), lambda b, g: (b * ng + g, 0)))

    slabs = [slab_specs(w) for w in later_weights]
    return pl.pallas_call(
        _attn_kernel,
        grid=(B, ng),
        in_specs=[
            pl.BlockSpec((1, H, Q_BLOCKS, LANES, blk), lambda b, g: (b, 0, g, 0, 0)),
            pl.BlockSpec((1, S, ATTN_WIDTH), lambda b, g: (b, 0, 0)),
            pl.BlockSpec((1, H, nb, V_ROWS, blk), lambda b, g: (b, 0, 0, 0, 0)),
            pl.BlockSpec((1, nb, ATTN_WIDTH), lambda b, g: (b, 0, 0)),
        ] + [spec_in for spec_in, _ in slabs],
        out_specs=[pl.BlockSpec((1, H, Q_BLOCKS, HEAD_DIM, blk), lambda b, g: (b, 0, g, 0, 0))]
        + [spec_out for _, spec_out in slabs],
        out_shape=[jax.ShapeDtypeStruct((B, H, nb, HEAD_DIM, blk), _BF16)]
        + [jax.ShapeDtypeStruct(w.shape[1:], _BF16) for w in later_weights],
        scratch_shapes=[
            pltpu.VMEM((n_chains, V_ROWS, blk), _F32),
            pltpu.VMEM((n_chains, 1, blk), _F32),
            pltpu.VMEM((n_chains, nb + 1, 1, blk), _F32),
            pltpu.VMEM((n_chains, blk, blk), _F32),
            pltpu.VMEM((n_chains, 1, blk), _F32),
            pltpu.VMEM((n_chains, blk, blk), _BF16),
        ],
        compiler_params=pltpu.CompilerParams(
            dimension_semantics=("arbitrary", "arbitrary"), vmem_limit_bytes=VMEM_LIMIT),
        name="moba_attn",
    )(qT, k, vT, kmean, *later_weights)


def _out_kernel(x_ref, oT_ref, hg0_ref, hgn_ref, halon_ref, dww_ref, dwb_ref, lng_ref, lnb_ref, wout_ref,
                g2_ref, wg_hbm, wu_hbm, wd_hbm, y_ref, hbuf_ref, hshift_ref, wtap_ref, cact_ref,
                wg_ref, wu_ref, wd_ref, wsem, *, tiles_per_seq):
    tm = x_ref.shape[1]
    nib = tm // MOBA_BLOCK
    n = pl.program_id(0)

    n_chunks = tm // CONV_ROWS

    def exact_zero(v):
        return jnp.minimum(jnp.abs(v[0:SUBLANES, 0:LANES]), 0.0)

    def add_to_corner(v, z):
        top = jnp.concatenate([v[0:SUBLANES, 0:LANES] + z, v[0:SUBLANES, LANES:]], axis=1)
        return top if v.shape[0] == SUBLANES else jnp.concatenate([top, v[SUBLANES:]], axis=0)

    def conv_setup(hg_tile, halo):
        hbuf_ref[0:CONV_HALO, :] = halo
        hbuf_ref[CONV_HALO:CONV_HALO + tm, :] = hg_tile
        span = hshift_ref.shape[1]
        for r in range(1, SUBLANES):
            hshift_ref[r - 1] = hbuf_ref[r:r + span, :]
        for kk in range(CONV_KERNEL):
            wtap_ref[kk] = jnp.broadcast_to(dww_ref[kk:kk + 1, :], (SUBLANES, CONV_WIDTH))

    def conv_chunk(ci, after=None):
        c0 = ci * CONV_ROWS
        first = CONV_HALO - (CONV_KERNEL - 1)
        accs = [None] * (CONV_ROWS // SUBLANES)
        for kk in range(CONV_KERNEL):
            r = (first + kk) % SUBLANES
            lo = c0 + first + kk - r
            src = hbuf_ref if r == 0 else hshift_ref.at[r - 1]
            wk = wtap_ref[kk]
            for gi in range(len(accs)):
                term = wk * src[lo + gi * SUBLANES:lo + (gi + 1) * SUBLANES, :]
                if accs[gi] is None:
                    accs[gi] = term if after is None else add_to_corner(term, after)
                else:
                    accs[gi] = accs[gi] + term
        conv = jnp.concatenate(accs, axis=0) + dwb_ref[...]
        mu = jnp.mean(conv, axis=-1, keepdims=True)
        cen = conv - mu
        var = jnp.mean(cen * cen, axis=-1, keepdims=True)
        cn = cen * lax.rsqrt(var + EPS) * lng_ref[...] + lnb_ref[...]
        cact_ref[c0:c0 + CONV_ROWS, :] = _silu(cn).astype(_BF16)
        return exact_zero(cn)

    @pl.when(n == 0)
    def _first_tile():
        copies = [pltpu.make_async_copy(src, dst, wsem.at[i])
                  for i, (src, dst) in enumerate(((wg_hbm, wg_ref), (wu_hbm, wu_ref), (wd_hbm, wd_ref)))]
        for cp in copies:
            cp.start()
        conv_setup(hg0_ref[0], jnp.zeros((CONV_HALO, CONV_WIDTH), _F32))
        for ci in range(n_chunks):
            conv_chunk(ci)
        for cp in copies:
            cp.wait()

    cact = cact_ref[...]
    nxt = jnp.minimum(n + 1, pl.num_programs(0) - 1)
    halo = halon_ref[0]
    conv_setup(hgn_ref[0], jnp.where(nxt % tiles_per_seq == 0, jnp.zeros_like(halo), halo))

    finished = []

    def tied_dot(a, b):
        d = len(finished)
        out = _dot(a, b)
        if d < n_chunks:
            out = add_to_corner(out, conv_chunk(d, finished[d - CONV_CHAINS] if d >= CONV_CHAINS else None))
        finished.append(exact_zero(out))
        return out

    mix = _dot(cact, wout_ref[ATTN_WIDTH:ATTN_WIDTH + CONV_WIDTH, :])
    attn_parts = []
    for ib in range(nib):
        oT = oT_ref[0, :, ib].reshape(ATTN_WIDTH, MOBA_BLOCK)
        attn_parts.append(lax.dot_general(oT, wout_ref[0:ATTN_WIDTH, :], (((0,), (0,)), ((), ())),
                                          preferred_element_type=_F32))
    x1 = x_ref[0] + (mix + jnp.concatenate(attn_parts, axis=0))

    ms = jnp.mean(x1 * x1, axis=-1, keepdims=True)
    h2 = (x1 * lax.rsqrt(ms + EPS) * g2_ref[...]).astype(_BF16)
    y = x1
    for c0 in range(0, D_FF, FF_CHUNK):
        cs = slice(c0, c0 + FF_CHUNK)
        gt = tied_dot(h2, wg_ref[:, cs])
        up = tied_dot(h2, wu_ref[:, cs])
        act = (_silu(gt) * up).astype(_BF16)
        y = y + tied_dot(act, wd_ref[cs, :])
    assert len(finished) >= n_chunks, "every conv chunk must be tied to a matmul"
    y_ref[0] = y


def _out_call(x, oT, hg, dww, dwb, lng, lnb, wout, g2, wg, wu, wd):
    B, S, D = x.shape
    tm = TM_OUT
    nib = tm // MOBA_BLOCK
    H = N_HEADS
    const = lambda n: (0, 0)
    resident = functools.partial(pl.BlockSpec, pipeline_mode=pl.Buffered(1))
    halo_blocks = tm // CONV_HALO
    nt = S // tm
    n_steps = B * nt

    def next_tile(n):
        nxt = jnp.minimum(n + 1, n_steps - 1)
        return nxt // nt, nxt % nt

    def next_hg(n):
        b, t = next_tile(n)
        return b, t, 0

    def next_halo(n):
        b, t = next_tile(n)
        return b, jnp.maximum(t * halo_blocks - 1, 0), 0

    return pl.pallas_call(
        functools.partial(_out_kernel, tiles_per_seq=nt),
        grid=(n_steps,),
        in_specs=[
            pl.BlockSpec((1, tm, D), lambda n: (n // nt, n % nt, 0)),
            pl.BlockSpec((1, H, nib, HEAD_DIM, MOBA_BLOCK), lambda n: (n // nt, 0, n % nt, 0, 0)),
            pl.BlockSpec((1, tm, CONV_WIDTH), lambda n: (0, 0, 0)),
            pl.BlockSpec((1, tm, CONV_WIDTH), next_hg),
            pl.BlockSpec((1, CONV_HALO, CONV_WIDTH), next_halo),
            pl.BlockSpec((CONV_KERNEL, CONV_WIDTH), const),
            pl.BlockSpec((1, CONV_WIDTH), const),
            pl.BlockSpec((1, CONV_WIDTH), const),
            pl.BlockSpec((1, CONV_WIDTH), const),
            resident((D, D), const),
            pl.BlockSpec((1, D), const),
            pl.BlockSpec(memory_space=pl.ANY),
            pl.BlockSpec(memory_space=pl.ANY),
            pl.BlockSpec(memory_space=pl.ANY),
        ],
        out_specs=pl.BlockSpec((1, tm, D), lambda n: (n // nt, n % nt, 0)),
        out_shape=jax.ShapeDtypeStruct((B, S, D), _F32),
        scratch_shapes=[
            pltpu.VMEM((CONV_HALO + tm, CONV_WIDTH), _F32),
            pltpu.VMEM((SUBLANES - 1, CONV_HALO + tm - SUBLANES, CONV_WIDTH), _F32),
            pltpu.VMEM((CONV_KERNEL, SUBLANES, CONV_WIDTH), _F32),
            pltpu.VMEM((tm, CONV_WIDTH), _BF16),
            pltpu.VMEM((D, D_FF), _BF16),
            pltpu.VMEM((D, D_FF), _BF16),
            pltpu.VMEM((D_FF, D), _BF16),
            pltpu.SemaphoreType.DMA((3,)),
        ],
        compiler_params=pltpu.CompilerParams(
            dimension_semantics=("arbitrary",), vmem_limit_bytes=VMEM_LIMIT),
        name="moba_out_ffn",
    )(x, oT, hg, hg, hg, dww, dwb, lng, lnb, wout, g2, wg, wu, wd)


def _rope_tables(seq_len):
    pos = np.arange(seq_len, dtype=np.float64)
    inv_freq = ROPE_THETA ** (-np.arange(0, HEAD_DIM, 2, dtype=np.float64) / HEAD_DIM)
    ang = pos[:, None] * inv_freq[None, :]
    ang = np.concatenate([ang, ang], axis=-1)
    sign = np.where(np.arange(HEAD_DIM) < HEAD_DIM // 2, -1.0, 1.0)
    cos2 = np.tile(np.cos(ang), (1, HEADS_PER_VREG))
    sin2 = np.tile(np.sin(ang) * sign[None, :], (1, HEADS_PER_VREG))
    return jnp.asarray(cos2, _F32), jnp.asarray(sin2, _F32)


def _layer(x, layer, norm1_g, w_in, glu_b, q_norm_g, k_norm_g, dw_w, dw_b, conv_ln_g, conv_ln_b,
           w_out, norm2_g, w_gate, w_up, w_down, cos2, sin2, gmat):
    B, S, _ = x.shape
    row = lambda a: a.reshape(1, -1)
    qT, k, vT, kmean, hg = _proj_call(
        x, row(norm1_g), w_in, row(glu_b),
        row(jnp.tile(q_norm_g, N_HEADS)), row(jnp.tile(k_norm_g, N_HEADS)), gmat, cos2, sin2, layer)
    kmean = kmean.reshape(B, S // MOBA_BLOCK, ATTN_WIDTH)
    oT, w_out_bf, w_gate_bf, w_up_bf, w_down_bf = _attn_call(
        qT, k, vT, kmean, layer, [w_out, w_gate, w_up, w_down])
    return _out_call(x, oT, hg, dw_w, row(dw_b), row(conv_ln_g), row(conv_ln_b),
                     w_out_bf, row(norm2_g), w_gate_bf, w_up_bf, w_down_bf)


def kernel(x, norm1_g, w_in, glu_b, q_norm_g, k_norm_g, dw_w, dw_b, conv_ln_g, conv_ln_b, w_out,
           norm2_g, w_gate, w_up, w_down):
    S = x.shape[1]
    cos2, sin2 = _rope_tables(S)
    head_of = np.arange(ATTN_WIDTH) // HEAD_DIM
    gmat = jnp.asarray(np.where(head_of[:, None] == head_of[None, :], 1.0 / HEAD_DIM, 0.0), _BF16)
    for l in range(norm1_g.shape[0]):
        x = _layer(x, l, norm1_g[l], w_in, glu_b[l], q_norm_g[l], k_norm_g[l], dw_w[l], dw_b[l],
                   conv_ln_g[l], conv_ln_b[l], w_out, norm2_g[l], w_gate, w_up, w_down,
                   cos2, sin2, gmat)
    return x
```
